```python
import math
import jax, jax.numpy as jnp
from jax import lax
import numpy as np

D_MODEL = 1024
BATCH = 4
SEQ = 8192
DEPTH = 1
DEC_BATCH = 128
DEC_SEQ = 8
PAST_LEN = 16384
PAGE_SIZE = 128

N_HEADS = 8
NOPE_DIM = 64
ROPE_DIM = 32
V_HEAD = 64
Q_LORA = 384
KV_LORA = 256
ATTN_WIDTH = N_HEADS * V_HEAD
CONV_CH = 512
CONV_K = 31
IN_WIDTH = Q_LORA + KV_LORA + ROPE_DIM + 2 * CONV_CH
MIX_WIDTH = ATTN_WIDTH + CONV_CH
D_FF = ((8 * D_MODEL // 3 + 255) // 256) * 256
PLE_DIM = 256
ROPE_THETA = 10000.0
Q_BLOCK = 128
EPS = 1e-6
NEG_INF = -1e30
SCALE = 1.0 / math.sqrt(NOPE_DIM + ROPE_DIM)

kernel_name = "hymba_conformer_mla_decoder_step"

F32 = jnp.float32


def rms_norm(x, g):
    xf = x.astype(F32)
    y = xf * lax.rsqrt(jnp.mean(xf * xf, axis=-1, keepdims=True) + EPS)
    return (y * g.astype(F32)).astype(x.dtype)


def layer_norm(x, g, b):
    xf = x.astype(F32)
    mu = jnp.mean(xf, axis=-1, keepdims=True)
    var = jnp.mean(jnp.square(xf - mu), axis=-1, keepdims=True)
    y = (xf - mu) * lax.rsqrt(var + EPS) * g.astype(F32) + b.astype(F32)
    return y.astype(x.dtype)


def rope_tables(pos):
    inv = 1.0 / (ROPE_THETA ** (jnp.arange(0, ROPE_DIM, 2, dtype=F32) / ROPE_DIM))
    ang = pos.astype(F32)[:, None] * inv[None, :]
    return jnp.cos(ang), jnp.sin(ang)


def apply_rope(x, cos, sin):
    x1, x2 = jnp.split(x.astype(F32), 2, axis=-1)
    return jnp.concatenate([x1 * cos - x2 * sin, x2 * cos + x1 * sin], axis=-1).astype(x.dtype)


def mixer_inputs(h, pos, lp):
    z = h @ lp["w_in"]
    c_q, c_kv, k_pe, u = jnp.split(z, [Q_LORA, Q_LORA + KV_LORA, Q_LORA + KV_LORA + ROPE_DIM], axis=-1)
    c_q = rms_norm(c_q, lp["q_lora_norm"])
    q = (c_q @ lp["w_uq"]).reshape(*c_q.shape[:-1], N_HEADS, NOPE_DIM + ROPE_DIM)
    cos, sin = rope_tables(pos)
    q_nope = rms_norm(q[..., :NOPE_DIM], lp["q_nope_norm"])
    q_pe = apply_rope(rms_norm(q[..., NOPE_DIM:], lp["q_pe_norm"]), cos[:, None], sin[:, None])
    c_kv = rms_norm(c_kv, lp["kv_norm"])
    k_pe = apply_rope(rms_norm(k_pe, lp["k_pe_norm"]), cos, sin)
    return q_nope, q_pe, c_kv, k_pe, u


def expand_kv(c_kv, lp):
    k_nope = (c_kv @ lp["w_uk"]).reshape(*c_kv.shape[:-1], N_HEADS, NOPE_DIM)
    k_nope = rms_norm(k_nope, lp["k_nope_norm"])
    v = (c_kv @ lp["w_uv"]).reshape(*c_kv.shape[:-1], N_HEADS, V_HEAD)
    return k_nope, v


def attn_scores(q_nope, q_pe, k_nope, k_pe):
    s = jnp.einsum("bqhd,bkhd->bhqk", q_nope, k_nope).astype(F32)
    s = s + jnp.einsum("bqhr,bkr->bhqk", q_pe, k_pe).astype(F32)
    return s * SCALE


def attend_prompt(q_nope, q_pe, c_kv, k_pe, lp):
    B, T = q_nope.shape[:2]
    k_nope, v = expand_kv(c_kv, lp)
    kpos = jnp.arange(T)

    def block(i):
        s0 = i * Q_BLOCK
        qn = lax.dynamic_slice_in_dim(q_nope, s0, Q_BLOCK, axis=1)
        qp = lax.dynamic_slice_in_dim(q_pe, s0, Q_BLOCK, axis=1)
        sc = attn_scores(qn, qp, k_nope, k_pe)
        qpos = s0 + jnp.arange(Q_BLOCK)
        sc = jnp.where(kpos[None, :] <= qpos[:, None], sc, NEG_INF)
        p = jax.nn.softmax(sc, axis=-1)
        return jnp.einsum("bhqk,bkhd->bqhd", p.astype(v.dtype), v)

    o = lax.map(block, jnp.arange(T // Q_BLOCK))
    return jnp.moveaxis(o, 0, 1).reshape(B, T, ATTN_WIDTH)


def online_update(carry, sc, v):
    m, l, acc = carry
    m_new = jnp.maximum(m, jnp.max(sc, axis=-1))
    alpha = jnp.exp(m - m_new)
    p = jnp.exp(sc - m_new[..., None])
    l = l * alpha + jnp.sum(p, axis=-1)
    acc = acc * alpha[..., None] + jnp.einsum("bhqk,bkhd->bhqd", p, v.astype(F32))
    return (m_new, l, acc)


def attend_sample(q_nope, q_pe, c_kv, k_pe, cache_ckv, cache_kpe, page_table, layer, lp):
    Bd, Tq = q_nope.shape[:2]
    init = (jnp.full((Bd, N_HEADS, Tq), NEG_INF, F32),
            jnp.zeros((Bd, N_HEADS, Tq), F32),
            jnp.zeros((Bd, N_HEADS, Tq, V_HEAD), F32))

    def page_step(carry, pages):
        ckv = cache_ckv[layer, pages]
        kpe = cache_kpe[layer, pages]
        k_nope, v = expand_kv(ckv.astype(q_nope.dtype), lp)
        sc = attn_scores(q_nope, q_pe, k_nope, kpe.astype(q_pe.dtype))
        return online_update(carry, sc, v), None

    carry, _ = lax.scan(page_step, init, page_table.T)
    k_nope, v = expand_kv(c_kv, lp)
    sc = attn_scores(q_nope, q_pe, k_nope, k_pe)
    idx = jnp.arange(Tq)
    sc = jnp.where(idx[None, :] <= idx[:, None], sc, NEG_INF)
    m, l, acc = online_update(carry, sc, v)
    o = (acc / l[..., None]).astype(q_nope.dtype)
    return jnp.transpose(o, (0, 2, 1, 3)).reshape(Bd, Tq, ATTN_WIDTH)


def conv_mixer(u, conv_state, lp):
    a, g = jnp.split(u, 2, axis=-1)
    glu = a * jax.nn.sigmoid(g)
    xc = jnp.concatenate([conv_state.astype(glu.dtype), glu], axis=1)
    y = lax.conv_general_dilated(xc, lp["conv_w"][:, None, :].astype(glu.dtype), (1,), "VALID",
                                 dimension_numbers=("NWC", "WIO", "NWC"),
                                 feature_group_count=CONV_CH) + lp["conv_b"]
    y = jax.nn.silu(layer_norm(y, lp["conv_ln_g"], lp["conv_ln_b"]))
    return y, xc[:, -(CONV_K - 1):]


def layer_forward(x, pe, pos, conv_state, attend, lp):
    h = rms_norm(x, lp["norm_mix"])
    q_nope, q_pe, c_kv, k_pe, u = mixer_inputs(h, pos, lp)
    attn = attend(q_nope, q_pe, c_kv, k_pe)
    conv, new_conv = conv_mixer(u, conv_state, lp)
    mix = jnp.concatenate([rms_norm(attn, lp["norm_out_attn"]), rms_norm(conv, lp["norm_out_conv"])], axis=-1)
    x = x + mix @ lp["w_out"]
    h = rms_norm(x, lp["norm_ffn"])
    x = x + (jax.nn.silu(h @ lp["w_gate"]) * (h @ lp["w_up"])) @ lp["w_down"]
    x = x + jax.nn.sigmoid(rms_norm(x, lp["norm_ple"]) @ lp["w_ple_gate"]) * (pe @ lp["w_ple_proj"])
    return x, c_kv, k_pe, new_conv


def setup_inputs(seed: int = 0) -> dict:
    key = jax.random.key(seed)
    ks = iter(jax.random.split(key, 48))

    def nrm(shape, scale=1.0):
        return jax.random.normal(next(ks), shape, F32) * scale

    def gain(n):
        return 1.0 + nrm((DEPTH, n), 0.05)

    n_pages = PAST_LEN // PAGE_SIZE
    n_used = DEC_BATCH * n_pages
    n_pool = n_used + n_used // 4
    page_table = jax.random.permutation(next(ks), n_pool)[:n_used].reshape(DEC_BATCH, n_pages).astype(jnp.int32)
    return {
        "x_prompt": nrm((BATCH, SEQ, D_MODEL)),
        "x_sample": nrm((DEC_BATCH, DEC_SEQ, D_MODEL)),
        "cache_ckv": nrm((DEPTH, n_pool, PAGE_SIZE, KV_LORA)),
        "cache_kpe": nrm((DEPTH, n_pool, PAGE_SIZE, ROPE_DIM)),
        "state_conv": nrm((DEPTH, DEC_BATCH, CONV_K - 1, CONV_CH), 0.5),
        "page_table": page_table,
        "p_prompt": nrm((DEPTH, BATCH, SEQ, PLE_DIM)),
        "p_sample": nrm((DEPTH, DEC_BATCH, DEC_SEQ, PLE_DIM)),
        "norm_mix": gain(D_MODEL),
        "w_in": nrm((DEPTH, D_MODEL, IN_WIDTH), D_MODEL ** -0.5),
        "q_lora_norm": gain(Q_LORA),
        "w_uq": nrm((DEPTH, Q_LORA, N_HEADS * (NOPE_DIM + ROPE_DIM)), Q_LORA ** -0.5),
        "kv_norm": gain(KV_LORA),
        "w_uk": nrm((DEPTH, KV_LORA, N_HEADS * NOPE_DIM), KV_LORA ** -0.5),
        "w_uv": nrm((DEPTH, KV_LORA, N_HEADS * V_HEAD), KV_LORA ** -0.5),
        "q_nope_norm": gain(NOPE_DIM),
        "k_nope_norm": gain(NOPE_DIM),
        "q_pe_norm": gain(ROPE_DIM),
        "k_pe_norm": gain(ROPE_DIM),
        "conv_w": nrm((DEPTH, CONV_K, CONV_CH), CONV_K ** -0.5),
        "conv_b": nrm((DEPTH, CONV_CH), 0.02),
        "conv_ln_g": gain(CONV_CH),
        "conv_ln_b": nrm((DEPTH, CONV_CH), 0.02),
        "norm_out_attn": gain(ATTN_WIDTH),
        "norm_out_conv": gain(CONV_CH),
        "w_out": nrm((DEPTH, MIX_WIDTH, D_MODEL), MIX_WIDTH ** -0.5),
        "norm_ffn": gain(D_MODEL),
        "w_gate": nrm((DEPTH, D_MODEL, D_FF), D_MODEL ** -0.5),
        "w_up": nrm((DEPTH, D_MODEL, D_FF), D_MODEL ** -0.5),
        "w_down": nrm((DEPTH, D_FF, D_MODEL), D_FF ** -0.5),
        "norm_ple": gain(D_MODEL),
        "w_ple_gate": nrm((DEPTH, D_MODEL, D_MODEL), D_MODEL ** -0.5),
        "w_ple_proj": nrm((DEPTH, PLE_DIM, D_MODEL), PLE_DIM ** -0.5),
    }


def reference(x_prompt, x_sample, cache_ckv, cache_kpe, state_conv, page_table, p_prompt, p_sample,
              norm_mix, w_in, q_lora_norm, w_uq, kv_norm, w_uk, w_uv, q_nope_norm, k_nope_norm,
              q_pe_norm, k_pe_norm, conv_w, conv_b, conv_ln_g, conv_ln_b, norm_out_attn, norm_out_conv,
              w_out, norm_ffn, w_gate, w_up, w_down, norm_ple, w_ple_gate, w_ple_proj):
    B, T = x_prompt.shape[:2]
    Bd, Tq = x_sample.shape[:2]
    pos_prompt = jnp.arange(T)
    pos_sample = PAST_LEN + jnp.arange(Tq)
    zero_conv = jnp.zeros((B, CONV_K - 1, CONV_CH), x_prompt.dtype)
    xp, xs = x_prompt, x_sample
    ckv_p, kpe_p, conv_p, ckv_s, kpe_s, conv_s = [], [], [], [], [], []
    for i in range(DEPTH):
        lp = {
            "norm_mix": norm_mix[i], "w_in": w_in[i], "q_lora_norm": q_lora_norm[i], "w_uq": w_uq[i],
            "kv_norm": kv_norm[i], "w_uk": w_uk[i], "w_uv": w_uv[i], "q_nope_norm": q_nope_norm[i],
            "k_nope_norm": k_nope_norm[i], "q_pe_norm": q_pe_norm[i], "k_pe_norm": k_pe_norm[i],
            "conv_w": conv_w[i], "conv_b": conv_b[i], "conv_ln_g": conv_ln_g[i], "conv_ln_b": conv_ln_b[i],
            "norm_out_attn": norm_out_attn[i], "norm_out_conv": norm_out_conv[i], "w_out": w_out[i],
            "norm_ffn": norm_ffn[i], "w_gate": w_gate[i], "w_up": w_up[i], "w_down": w_down[i],
            "norm_ple": norm_ple[i], "w_ple_gate": w_ple_gate[i], "w_ple_proj": w_ple_proj[i],
        }
        att_p = lambda qn, qp, ckv, kpe, lp=lp: attend_prompt(qn, qp, ckv, kpe, lp)
        att_s = lambda qn, qp, ckv, kpe, lp=lp, i=i: attend_sample(qn, qp, ckv, kpe, cache_ckv, cache_kpe,
                                                                   page_table, i, lp)
        xp, c1, k1, v1 = layer_forward(xp, p_prompt[i], pos_prompt, zero_conv, att_p, lp)
        xs, c2, k2, v2 = layer_forward(xs, p_sample[i], pos_sample, state_conv[i], att_s, lp)
        ckv_p.append(c1); kpe_p.append(k1); conv_p.append(v1)
        ckv_s.append(c2); kpe_s.append(k2); conv_s.append(v2)
    return (xp, xs, jnp.stack(ckv_p), jnp.stack(kpe_p), jnp.stack(conv_p),
            jnp.stack(ckv_s), jnp.stack(kpe_s), jnp.stack(conv_s))
```

```python
import functools
import math

import jax
import jax.numpy as jnp
from jax import lax
from jax.experimental import pallas as pl
from jax.experimental.pallas import tpu as pltpu

F32 = jnp.float32
BF16 = jnp.bfloat16
EPS = 1e-6
ROPE_THETA = 10000.0
LOG2E = math.log2(math.e)
NEG_INF = -1e30

LANES = 128
MXU_DIM = 256
VMEM_LIMIT = 56 * 1024 * 1024

TOKEN_TILE = 512
FLASH_TILE = 512
CONV_ROWS = 64
CONV_HALO = 32
PAGES_PER_CHUNK = 4


def _rms(x, g):
    return x * lax.rsqrt(jnp.mean(x * x, axis=-1, keepdims=True) + EPS) * g


def _dot(a, b):
    return jnp.dot(a, b, preferred_element_type=F32)


def _dot_t(a, b):
    return lax.dot_general(a, b, (((1,), (1,)), ((), ())), preferred_element_type=F32)


def _seg_ms(x, s_ref):
    x2 = (x * x).astype(BF16)
    w = s_ref.shape[0]
    parts = [_dot(x2[:, j * w:(j + 1) * w], s_ref[...]) for j in range(x.shape[1] // w)]
    return parts[0] if len(parts) == 1 else jnp.concatenate(parts, axis=1)


def _in_body(x_ref, tabq_ref, tabk_ref, gmix_ref, win_ref, gql_ref, wuq_ref, gqv_ref, sq_ref,
             gkv_ref, wukv_ref, gkv_vec_ref, sk_ref, gkpe_ref, skpe_ref,
             q_ref, k_ref, v_ref, ckv_ref, kpe_ref, glu_ref, *, ql, kl, cc, nh, rope):
    x = x_ref[0]
    h = _rms(x, gmix_ref[...]).astype(BF16)
    z = _dot(h, win_ref[...])
    o_a = ql + kl
    o_g = o_a + cc
    o_k = o_g + cc
    glu_ref[0] = z[:, o_a:o_g] * jax.nn.sigmoid(z[:, o_g:o_k])

    cqn = _rms(z[:, :ql], gql_ref[...]).astype(BF16)
    qa = _dot(cqn, wuq_ref[...])
    tq = jnp.concatenate([tabq_ref[...]] * nh, axis=1)
    q = qa * lax.rsqrt(_seg_ms(qa, sq_ref) + EPS) * gqv_ref[...] * tq
    q_ref[0] = q.astype(q_ref.dtype)

    ckvn = _rms(z[:, ql:o_a], gkv_ref[...])
    ckv_ref[0] = ckvn
    kv = _dot(ckvn.astype(BF16), wukv_ref[...])
    ka = kv[:, :nh * LANES]
    v_ref[0] = kv[:, nh * LANES:].astype(v_ref.dtype)
    kn = ka * lax.rsqrt(_seg_ms(ka, sk_ref) + EPS) * gkv_vec_ref[...]

    kp = z[:, o_k:o_k + LANES]
    t = kp * lax.rsqrt(_seg_ms(kp, skpe_ref) + EPS) * gkpe_ref[...] * tabk_ref[...]
    kd = t + pltpu.roll(t, rope, axis=1)
    kpe_ref[0] = kd[:, :rope]
    lane = lax.broadcasted_iota(jnp.int32, kd.shape, 1)
    kdm = jnp.where(lane >= LANES - 2 * rope, kd, 0.0)
    k_ref[0] = (kn + jnp.concatenate([kdm] * nh, axis=1)).astype(k_ref.dtype)


def _in_proj(x, tabq, tabk, p, q_dtype):
    B, T, D = x.shape
    tm = min(TOKEN_TILE, T)
    nh, ql, kl, cc, rope, vw = p["nh"], p["ql"], p["kl"], p["cc"], p["rope"], p["vw"]
    const = lambda a: pl.BlockSpec(a.shape, lambda b, i: (0,) * a.ndim, pipeline_mode=pl.Buffered(1))
    tok = lambda w: pl.BlockSpec((1, tm, w), lambda b, i: (b, i, 0))
    tab = pl.BlockSpec((tm, LANES), lambda b, i: (i, 0))
    consts = [p["g_mix"], p["w_in"], p["g_ql"], p["w_uq"], p["g_qvec"], p["s_q"],
              p["g_kv"], p["w_ukv"], p["g_kvec"], p["s_k"], p["g_kpe"], p["s_kpe"]]
    out_shape = (
        jax.ShapeDtypeStruct((B, T, nh * LANES), q_dtype),
        jax.ShapeDtypeStruct((B, T, nh * LANES), BF16),
        jax.ShapeDtypeStruct((B, T, vw), BF16),
        jax.ShapeDtypeStruct((B, T, kl), F32),
        jax.ShapeDtypeStruct((B, T, rope), F32),
        jax.ShapeDtypeStruct((B, T, cc), F32),
    )
    return pl.pallas_call(
        functools.partial(_in_body, ql=ql, kl=kl, cc=cc, nh=nh, rope=rope),
        grid=(B, T // tm),
        in_specs=[tok(D), tab, tab] + [const(a) for a in consts],
        out_specs=[tok(nh * LANES), tok(nh * LANES), tok(vw), tok(kl), tok(rope), tok(cc)],
        out_shape=out_shape,
        compiler_params=pltpu.CompilerParams(
            dimension_semantics=("parallel", "parallel"), vmem_limit_bytes=VMEM_LIMIT),
        name="in_proj",
    )(x, tabq, tabk, *consts)


def _flash_body(q_ref, k_ref, v_ref, o_ref, m_ref, l_ref, acc_ref, *, tile, vh):
    i = pl.program_id(2)
    row = lax.broadcasted_iota(jnp.int32, (tile, tile), 0)
    col = lax.broadcasted_iota(jnp.int32, (tile, tile), 1)
    outs = []
    for h in range(2):
        q = q_ref[0, :, h * LANES:(h + 1) * LANES]
        m_ref[...] = jnp.full(m_ref.shape, NEG_INF, F32)
        l_ref[...] = jnp.zeros(l_ref.shape, F32)
        acc_ref[...] = jnp.zeros(acc_ref.shape, F32)

        def step(j, masked, q=q, h=h):
            start = pl.multiple_of(j * tile, tile)
            k = k_ref[0, pl.ds(start, tile), h * LANES:(h + 1) * LANES]
            s = _dot_t(q, k)
            if masked:
                s = jnp.where(col <= row, s, NEG_INF)
            m_prev = m_ref[...]
            m_new = jnp.maximum(m_prev, jnp.max(s, axis=-1, keepdims=True))
            alpha = jnp.exp2(m_prev - m_new)
            pr = jnp.exp2(s - m_new)
            l_ref[...] = alpha * l_ref[...] + jnp.sum(pr, axis=-1, keepdims=True)
            acc_ref[...] = alpha * acc_ref[...] + _dot(pr.astype(BF16), v_ref[0, pl.ds(start, tile), :])
            m_ref[...] = m_new

        def full_step(j, carry):
            step(j, False)
            return carry

        lax.fori_loop(0, i, full_step, 0)
        step(i, True)
        outs.append(acc_ref[...] / l_ref[...])
    lane = lax.broadcasted_iota(jnp.int32, outs[0].shape, 1)
    o_ref[0] = jnp.where(lane < vh, outs[0], outs[1])


def _flash(q, k, v, vh):
    B, T, W = q.shape
    nhp = W // (2 * LANES)
    tile = min(FLASH_TILE, T)
    return pl.pallas_call(
        functools.partial(_flash_body, tile=tile, vh=vh),
        grid=(B, nhp, T // tile),
        in_specs=[
            pl.BlockSpec((1, tile, 2 * LANES), lambda b, hp, i: (b, i, hp)),
            pl.BlockSpec((1, T, 2 * LANES), lambda b, hp, i: (b, 0, hp)),
            pl.BlockSpec((1, T, 2 * vh), lambda b, hp, i: (b, 0, hp)),
        ],
        out_specs=pl.BlockSpec((1, tile, 2 * vh), lambda b, hp, i: (b, i, hp)),
        out_shape=jax.ShapeDtypeStruct((B, T, nhp * 2 * vh), F32),
        scratch_shapes=[pltpu.VMEM((tile, 1), F32), pltpu.VMEM((tile, 1), F32),
                        pltpu.VMEM((tile, 2 * vh), F32)],
        compiler_params=pltpu.CompilerParams(
            dimension_semantics=("parallel", "parallel", "arbitrary"), vmem_limit_bytes=VMEM_LIMIT),
        name="flash",
    )(q, k, v)


def _absorb_body(q_ref, gk_ref, wukt_ref, pmat_ref, a_ref, qpe_ref, *, nh, kl):
    q = q_ref[...]
    qpe_ref[...] = _dot(q.astype(BF16), pmat_ref[...]).astype(qpe_ref.dtype)
    qg = (q * gk_ref[...]).astype(BF16)
    for h in range(nh):
        a_ref[:, h * kl:(h + 1) * kl] = _dot(qg[:, h * LANES:(h + 1) * LANES], wukt_ref[h]).astype(a_ref.dtype)


def _absorb(q, p):
    N, W = q.shape
    nh, kl, rope = p["nh"], p["kl"], p["rope"]
    tm = min(256, N)
    const = lambda a: pl.BlockSpec(a.shape, lambda i: (0,) * a.ndim, pipeline_mode=pl.Buffered(1))
    consts = [p["g_kvec"], p["w_ukt_pad"], p["p_mat"]]
    return pl.pallas_call(
        functools.partial(_absorb_body, nh=nh, kl=kl),
        grid=(N // tm,),
        in_specs=[pl.BlockSpec((tm, W), lambda i: (i, 0))] + [const(a) for a in consts],
        out_specs=[pl.BlockSpec((tm, nh * kl), lambda i: (i, 0)),
                   pl.BlockSpec((tm, nh * rope), lambda i: (i, 0))],
        out_shape=(jax.ShapeDtypeStruct((N, nh * kl), BF16), jax.ShapeDtypeStruct((N, nh * rope), BF16)),
        compiler_params=pltpu.CompilerParams(dimension_semantics=("parallel",), vmem_limit_bytes=VMEM_LIMIT),
        name="absorb",
    )(q, *consts)


def _paged_body(pt_ref, a_ref, qpe_ref, wukt_ref, cnew_ref, knew_ref, cache_c_ref, cache_k_ref, o_ref,
                cbuf, kbuf, sem, m_ref, l_ref, acc_ref, *, layer, n_pages, page, nh, nope):
    b = pl.program_id(0)
    nb = pl.num_programs(0)
    ch = PAGES_PER_CHUNK
    n_chunks = n_pages // ch
    rows = a_ref.shape[1]

    def copies(bb, cc, slot):
        out = []
        for i in range(ch):
            pg = pt_ref[bb * n_pages + cc * ch + i]
            dst = pl.ds(i * page, page)
            out.append(pltpu.make_async_copy(cache_c_ref.at[layer, pg], cbuf.at[slot, dst], sem.at[slot, 0]))
            out.append(pltpu.make_async_copy(cache_k_ref.at[layer, pg], kbuf.at[slot, dst], sem.at[slot, 1]))
        return out

    def fetch(bb, cc, slot):
        for c in copies(bb, cc, slot):
            c.start()

    @pl.when(b == 0)
    def _():
        fetch(0, 0, 0)

    m_ref[...] = jnp.full(m_ref.shape, NEG_INF, F32)
    l_ref[...] = jnp.zeros(l_ref.shape, F32)
    acc_ref[...] = jnp.zeros(acc_ref.shape, F32)
    a = a_ref[0]
    qpe = qpe_ref[0]

    def process(ckv, kpe, mask):
        cb = ckv.astype(BF16)
        kt = _dot_t(wukt_ref[...], cb)
        ss = jnp.sum((kt * kt).reshape(nope, nh, kt.shape[1]), axis=0)
        r = lax.rsqrt(ss * (1.0 / nope) + EPS)
        s = _dot_t(a, cb) * jnp.concatenate([r] * (rows // nh), axis=0) + _dot_t(qpe, kpe.astype(BF16))
        if mask is not None:
            s = jnp.where(mask, s, NEG_INF)
        m_prev = m_ref[...]
        m_new = jnp.maximum(m_prev, jnp.max(s, axis=-1, keepdims=True))
        alpha = jnp.exp2(m_prev - m_new)
        pr = jnp.exp2(s - m_new)
        l_ref[...] = alpha * l_ref[...] + jnp.sum(pr, axis=-1, keepdims=True)
        acc_ref[...] = alpha * acc_ref[...] + _dot(pr.astype(BF16), cb)
        m_ref[...] = m_new

    def chunk(c, carry):
        slot = c % 2

        @pl.when(c + 1 < n_chunks)
        def _():
            fetch(b, c + 1, 1 - slot)

        @pl.when(jnp.logical_and(c + 1 == n_chunks, b + 1 < nb))
        def _():
            fetch(b + 1, 0, 1 - slot)

        for cp in copies(b, c, slot):
            cp.wait()
        process(cbuf[slot], kbuf[slot], None)
        return carry

    lax.fori_loop(0, n_chunks, chunk, 0)

    tq = cnew_ref.shape[1]
    pad = page - tq
    cnew = jnp.concatenate([cnew_ref[0], jnp.zeros((pad, cnew_ref.shape[2]), F32)], axis=0)
    knew = jnp.concatenate([knew_ref[0], jnp.zeros((pad, knew_ref.shape[2]), F32)], axis=0)
    qi = lax.broadcasted_iota(jnp.int32, (rows, page), 0) // nh
    kj = lax.broadcasted_iota(jnp.int32, (rows, page), 1)
    process(cnew, knew, kj <= qi)
    o_ref[0] = acc_ref[...] / l_ref[...]


def _paged_attn(page_table, a, qpe, cnew, knew, cache_c, cache_k, p, layer):
    Bd, rows, kl = a.shape
    nh, rope, nope = p["nh"], p["rope"], p["nope"]
    n_pages = page_table.shape[1]
    page = cache_c.shape[2]
    assert n_pages % (2 * PAGES_PER_CHUNK) == 0, "chunk slots alternate per batch row"
    keys = PAGES_PER_CHUNK * page
    tq = cnew.shape[1]
    grid_spec = pltpu.PrefetchScalarGridSpec(
        num_scalar_prefetch=1,
        grid=(Bd,),
        in_specs=[
            pl.BlockSpec((1, rows, kl), lambda b, pt: (b, 0, 0)),
            pl.BlockSpec((1, rows, rope), lambda b, pt: (b, 0, 0)),
            pl.BlockSpec(p["w_ukt_perm"].shape, lambda b, pt: (0, 0), pipeline_mode=pl.Buffered(1)),
            pl.BlockSpec((1, tq, kl), lambda b, pt: (b, 0, 0)),
            pl.BlockSpec((1, tq, rope), lambda b, pt: (b, 0, 0)),
            pl.BlockSpec(memory_space=pl.ANY),
            pl.BlockSpec(memory_space=pl.ANY),
        ],
        out_specs=pl.BlockSpec((1, rows, kl), lambda b, pt: (b, 0, 0)),
        scratch_shapes=[
            pltpu.VMEM((2, keys, kl), F32),
            pltpu.VMEM((2, keys, rope), F32),
            pltpu.SemaphoreType.DMA((2, 2)),
            pltpu.VMEM((rows, 1), F32),
            pltpu.VMEM((rows, 1), F32),
            pltpu.VMEM((rows, kl), F32),
        ],
    )
    return pl.pallas_call(
        functools.partial(_paged_body, layer=layer, n_pages=n_pages, page=page, nh=nh, nope=nope),
        grid_spec=grid_spec,
        out_shape=jax.ShapeDtypeStruct((Bd, rows, kl), F32),
        compiler_params=pltpu.CompilerParams(dimension_semantics=("arbitrary",), vmem_limit_bytes=VMEM_LIMIT),
        name="paged_attn",
    )(page_table.reshape(-1), a, qpe, p["w_ukt_perm"], cnew, knew, cache_c, cache_k)


def _sproj_body(o_ref, w_ref, out_ref):
    out_ref[...] = _dot(o_ref[...].astype(BF16), w_ref[...])


def _sproj(o_flat, w_bd):
    N, K = o_flat.shape
    tm = min(TOKEN_TILE, N)
    return pl.pallas_call(
        _sproj_body,
        grid=(N // tm,),
        in_specs=[pl.BlockSpec((tm, K), lambda i: (i, 0)),
                  pl.BlockSpec(w_bd.shape, lambda i: (0, 0), pipeline_mode=pl.Buffered(1))],
        out_specs=pl.BlockSpec((tm, w_bd.shape[1]), lambda i: (i, 0)),
        out_shape=jax.ShapeDtypeStruct((N, w_bd.shape[1]), F32),
        compiler_params=pltpu.CompilerParams(dimension_semantics=("parallel",), vmem_limit_bytes=VMEM_LIMIT),
        name="sproj",
    )(o_flat, w_bd)


def _post_tail(y, x, attn, pe, refs, out_ref):
    (lng_ref, lnb_ref, goc_ref, goa_ref, woa_ref, woc_ref, gffn_ref, wg_ref, wu_ref, wd_ref,
     gple_ref, wpg_ref, wpp_ref) = refs
    mu = jnp.mean(y, axis=-1, keepdims=True)
    yc = y - mu
    yn = yc * lax.rsqrt(jnp.mean(yc * yc, axis=-1, keepdims=True) + EPS) * lng_ref[...] + lnb_ref[...]
    conv = yn * jax.nn.sigmoid(yn)
    x = x + _dot(_rms(attn, goa_ref[...]).astype(BF16), woa_ref[...])
    x = x + _dot(_rms(conv, goc_ref[...]).astype(BF16), woc_ref[...])
    h = _rms(x, gffn_ref[...]).astype(BF16)
    g = _dot(h, wg_ref[...])
    u = _dot(h, wu_ref[...])
    x = x + _dot((g * jax.nn.sigmoid(g) * u).astype(BF16), wd_ref[...])
    gate = jax.nn.sigmoid(_dot(_rms(x, gple_ref[...]).astype(BF16), wpg_ref[...]))
    out_ref[0] = x + gate * _dot(pe.astype(BF16), wpp_ref[...])


def _post_prompt_body(x_ref, attn_ref, pe_ref, glu_ref, prev_ref, cw_ref, cb_ref, *rest, taps):
    refs, out_ref, xs_ref, y_ref = rest[:-3], rest[-3], rest[-2], rest[-1]
    i = pl.program_id(1)
    tm = glu_ref.shape[1]
    xs_ref[0:CONV_HALO, :] = jnp.where(i > 0, prev_ref[0], 0.0)
    xs_ref[CONV_HALO:, :] = glu_ref[0]
    off = CONV_HALO - (taps - 1)
    for r0 in range(0, tm, CONV_ROWS):
        acc = jnp.broadcast_to(cb_ref[...], (CONV_ROWS, cb_ref.shape[1]))
        for k in range(taps):
            acc = acc + cw_ref[k:k + 1, :] * xs_ref[r0 + off + k:r0 + off + k + CONV_ROWS, :]
        y_ref[r0:r0 + CONV_ROWS, :] = acc
    _post_tail(y_ref[...], x_ref[0], attn_ref[0], pe_ref[0], refs, out_ref)


def _post_sample_body(x_ref, attn_ref, pe_ref, xc_ref, cw_ref, cb_ref, *rest, taps):
    refs, out_ref = rest[:-1], rest[-1]
    nb, _, cc = xc_ref.shape
    tq = x_ref.shape[1] // nb
    acc = jnp.broadcast_to(cb_ref[...].reshape(1, 1, cc), (nb, tq, cc))
    for k in range(taps):
        acc = acc + cw_ref[k:k + 1, :].reshape(1, 1, cc) * xc_ref[:, k:k + tq, :]
    _post_tail(acc.reshape(nb * tq, cc), x_ref[0], attn_ref[0], pe_ref[0], refs, out_ref)


def _post(x, attn, pe, conv_in, p, sample):
    B, T, D = x.shape
    tm = min(TOKEN_TILE, T)
    cc, taps = p["cc"], p["taps"]
    const = lambda a: pl.BlockSpec(a.shape, lambda b, i: (0,) * a.ndim, pipeline_mode=pl.Buffered(1))
    tok = lambda w: pl.BlockSpec((1, tm, w), lambda b, i: (b, i, 0))
    consts = [p["conv_w"], p["conv_b"], p["ln_g"], p["ln_b"], p["g_oc"], p["g_oa"], p["w_oa"], p["w_oc"],
              p["g_ffn"], p["w_gate"], p["w_up"], p["w_down"], p["g_ple"], p["w_pg"], p["w_pp"]]
    if sample:
        tq = conv_in.shape[1] - (taps - 1)
        nb = tm // tq
        conv_specs = [pl.BlockSpec((nb,) + conv_in.shape[1:], lambda b, i: (i, 0, 0))]
        conv_args = [conv_in]
        body = functools.partial(_post_sample_body, taps=taps)
        scratch = []
    else:
        per = tm // CONV_HALO
        conv_specs = [tok(cc),
                      pl.BlockSpec((1, CONV_HALO, cc), lambda b, i: (b, jnp.maximum(i * per - 1, 0), 0))]
        conv_args = [conv_in, conv_in]
        body = functools.partial(_post_prompt_body, taps=taps)
        scratch = [pltpu.VMEM((tm + CONV_HALO, cc), F32), pltpu.VMEM((tm, cc), F32)]
    return pl.pallas_call(
        body,
        grid=(B, T // tm),
        in_specs=[tok(D), tok(attn.shape[2]), tok(pe.shape[2])] + conv_specs + [const(a) for a in consts],
        out_specs=tok(D),
        out_shape=jax.ShapeDtypeStruct((B, T, D), F32),
        scratch_shapes=scratch,
        compiler_params=pltpu.CompilerParams(
            dimension_semantics=("parallel", "parallel"), vmem_limit_bytes=VMEM_LIMIT),
        name="post_sample" if sample else "post_prompt",
    )(x, attn, pe, *conv_args, *consts)


def _rope_tables(pos, rope):
    inv = 1.0 / (ROPE_THETA ** (jnp.arange(0, rope, 2, dtype=F32) / rope))
    ang = pos.astype(F32)[:, None] * inv[None, :]
    cos, sin = jnp.cos(ang), jnp.sin(ang)
    cos2 = jnp.concatenate([cos, cos], axis=1)
    sin2 = jnp.concatenate([-sin, sin], axis=1)
    ones = jnp.ones((pos.shape[0], LANES - 2 * rope), F32)
    tabq = jnp.concatenate([ones, cos2, sin2], axis=1)
    tabk = jnp.concatenate([cos2, sin2, cos2, sin2], axis=1)
    return tabq, tabk


def _swap_halves(w, axis):
    a, b = jnp.split(w, 2, axis=axis)
    return jnp.concatenate([b, a], axis=axis)


def _layer_params(i, norm_mix, w_in, q_lora_norm, w_uq, kv_norm, w_uk, w_uv, q_nope_norm, k_nope_norm,
                  q_pe_norm, k_pe_norm, conv_w, conv_b, conv_ln_g, conv_ln_b, norm_out_attn, norm_out_conv,
                  w_out, norm_ffn, w_gate, w_up, w_down, norm_ple, w_ple_gate, w_ple_proj):
    ql, kl = q_lora_norm.shape[1], kv_norm.shape[1]
    nope, rope = q_nope_norm.shape[1], q_pe_norm.shape[1]
    cc, taps = conv_w.shape[2], conv_w.shape[1]
    nh = w_uk.shape[2] // nope
    vh = w_uv.shape[2] // nh
    assert nope + 2 * rope == LANES and 4 * rope == LANES and 2 * vh == LANES and nh % 2 == 0
    assert taps - 1 <= CONV_HALO
    scale = LOG2E / math.sqrt(nope + rope)
    row = lambda v: v.reshape(1, -1).astype(F32)

    wi = w_in[i]
    w_q, w_kv, w_kpe, w_a, w_g = jnp.split(wi, [ql, ql + kl, ql + kl + rope, ql + kl + rope + cc], axis=1)
    w_kpe_sw = _swap_halves(w_kpe, 1)
    w_in_cat = jnp.concatenate([w_q, w_kv, w_a, w_g, w_kpe, w_kpe_sw, w_kpe, w_kpe_sw], axis=1).astype(BF16)

    wq3 = w_uq[i].reshape(ql, nh, nope + rope)
    wq_pe = wq3[:, :, nope:]
    w_uq_cat = jnp.concatenate([wq3[:, :, :nope], wq_pe, _swap_halves(wq_pe, 2)], axis=2)
    w_uq_cat = w_uq_cat.reshape(ql, nh * LANES).astype(BF16)
    gq = jnp.concatenate([q_nope_norm[i], q_pe_norm[i], _swap_halves(q_pe_norm[i], 0)]) * scale
    g_qvec = row(jnp.tile(gq, nh))

    wk3 = w_uk[i].reshape(kl, nh, nope)
    w_uk_pad = jnp.concatenate([wk3, jnp.zeros((kl, nh, LANES - nope), F32)], axis=2).reshape(kl, nh * LANES)
    w_ukv = jnp.concatenate([w_uk_pad, w_uv[i]], axis=1).astype(BF16)
    gk = jnp.concatenate([k_nope_norm[i], jnp.zeros((LANES - nope,), F32)])
    g_kvec = row(jnp.tile(gk, nh))
    g_kpe = row(jnp.tile(jnp.concatenate([k_pe_norm[i], _swap_halves(k_pe_norm[i], 0)]), 2))

    li = jnp.arange(LANES)
    blk_q = (jnp.where((li[:, None] < nope) & (li[None, :] < nope), 1.0 / nope, 0.0)
             + jnp.where((li[:, None] >= nope) & (li[:, None] < nope + rope) & (li[None, :] >= nope),
                         1.0 / rope, 0.0))
    blk_k = jnp.where((li[:, None] < nope) & (li[None, :] < nope), 1.0 / nope, 0.0)
    eye2 = jnp.eye(MXU_DIM // LANES, dtype=F32)
    s_q = jnp.kron(eye2, blk_q).astype(BF16)
    s_k = jnp.kron(eye2, blk_k).astype(BF16)
    s_kpe = jnp.where(li[:, None] < rope, 1.0 / rope, 0.0) * jnp.ones((1, LANES), F32)
    s_kpe = s_kpe.astype(BF16)

    w_ukt_pad = jnp.concatenate([jnp.transpose(wk3, (1, 2, 0)),
                                 jnp.zeros((nh, LANES - nope, kl), F32)], axis=1).astype(BF16)
    w_ukt_perm = jnp.transpose(wk3, (2, 1, 0)).reshape(nope * nh, kl).astype(BF16)
    src = jnp.arange(nh * LANES)
    lane_in = src % LANES
    dst = (src // LANES) * rope + (lane_in - nope) % rope
    p_mat = jnp.where((lane_in >= nope)[:, None] & (dst[:, None] == jnp.arange(nh * rope)[None, :]), 1.0, 0.0)
    p_mat = p_mat.astype(BF16)
    wv3 = w_uv[i].reshape(kl, nh, vh)
    w_bd = jnp.einsum("chv,hg->hcgv", wv3, jnp.eye(nh, dtype=F32)).reshape(nh * kl, nh * vh).astype(BF16)

    mw = norm_out_attn.shape[1]
    return dict(
        nh=nh, ql=ql, kl=kl, cc=cc, rope=rope, nope=nope, vh=vh, vw=nh * vh, taps=taps,
        g_mix=row(norm_mix[i]), w_in=w_in_cat, g_ql=row(q_lora_norm[i]), w_uq=w_uq_cat, g_qvec=g_qvec, s_q=s_q,
        g_kv=row(kv_norm[i]), w_ukv=w_ukv, g_kvec=g_kvec, s_k=s_k, g_kpe=g_kpe, s_kpe=s_kpe,
        w_ukt_pad=w_ukt_pad, w_ukt_perm=w_ukt_perm, p_mat=p_mat, w_bd=w_bd,
        conv_w=conv_w[i].astype(F32), conv_b=row(conv_b[i]), ln_g=row(conv_ln_g[i]), ln_b=row(conv_ln_b[i]),
        g_oc=row(norm_out_conv[i]), g_oa=row(norm_out_attn[i]),
        w_oa=w_out[i][:mw].astype(BF16), w_oc=w_out[i][mw:].astype(BF16),
        g_ffn=row(norm_ffn[i]), w_gate=w_gate[i].astype(BF16), w_up=w_up[i].astype(BF16),
        w_down=w_down[i].astype(BF16), g_ple=row(norm_ple[i]), w_pg=w_ple_gate[i].astype(BF16),
        w_pp=w_ple_proj[i].astype(BF16),
    )


def kernel(x_prompt, x_sample, cache_ckv, cache_kpe, state_conv, page_table, p_prompt, p_sample, norm_mix, w_in, q_lora_norm, w_uq, kv_norm, w_uk, w_uv, q_nope_norm, k_nope_norm, q_pe_norm, k_pe_norm, conv_w, conv_b, conv_ln_g, conv_ln_b, norm_out_attn, norm_out_conv, w_out, norm_ffn, w_gate, w_up, w_down, norm_ple, w_ple_gate, w_ple_proj):
    B, T, D = x_prompt.shape
    Bd, Tq, _ = x_sample.shape
    depth = w_in.shape[0]
    past_len = page_table.shape[1] * cache_ckv.shape[2]
    rope = q_pe_norm.shape[1]
    taps = conv_w.shape[1]
    tabq_p, tabk_p = _rope_tables(jnp.arange(T), rope)
    tabq_s, tabk_s = _rope_tables(past_len + jnp.arange(Tq), rope)
    tabq_s, tabk_s = jnp.tile(tabq_s, (Bd, 1)), jnp.tile(tabk_s, (Bd, 1))

    xp, xs = x_prompt, x_sample.reshape(1, Bd * Tq, D)
    outs = [[] for _ in range(6)]
    for i in range(depth):
        p = _layer_params(i, norm_mix, w_in, q_lora_norm, w_uq, kv_norm, w_uk, w_uv, q_nope_norm, k_nope_norm,
                          q_pe_norm, k_pe_norm, conv_w, conv_b, conv_ln_g, conv_ln_b, norm_out_attn,
                          norm_out_conv, w_out, norm_ffn, w_gate, w_up, w_down, norm_ple, w_ple_gate, w_ple_proj)
        nh, kl, cc = p["nh"], p["kl"], p["cc"]
        q, k, v, ckv, kpe, glu = _in_proj(xp, tabq_p, tabk_p, p, BF16)
        attn = _flash(q, k, v, p["vh"])
        xp = _post(xp, attn, p_prompt[i], glu, p, sample=False)
        outs[0].append(ckv)
        outs[1].append(kpe)
        outs[2].append(glu[:, T - (taps - 1):, :])
        q_s, _, _, ckv_s, kpe_s, glu_s = _in_proj(xs, tabq_s, tabk_s, p, F32)
        a_s, qpe_s = _absorb(q_s[0], p)
        ckv_s = ckv_s.reshape(Bd, Tq, kl)
        kpe_s = kpe_s.reshape(Bd, Tq, rope)
        o_lat = _paged_attn(page_table, a_s.reshape(Bd, Tq * nh, kl), qpe_s.reshape(Bd, Tq * nh, rope),
                            ckv_s, kpe_s, cache_ckv, cache_kpe, p, i)
        attn_s = _sproj(o_lat.reshape(Bd * Tq, nh * kl), p["w_bd"])
        xc = jnp.concatenate([state_conv[i], glu_s.reshape(Bd, Tq, cc)], axis=1)
        xs = _post(xs, attn_s[None], p_sample[i].reshape(1, Bd * Tq, -1), xc, p, sample=True)
        outs[3].append(ckv_s)
        outs[4].append(kpe_s)
        outs[5].append(xc[:, Tq:, :])
    return (xp, xs.reshape(Bd, Tq, D), jnp.stack(outs[0]), jnp.stack(outs[1]), jnp.stack(outs[2]),
            jnp.stack(outs[3]), jnp.stack(outs[4]), jnp.stack(outs[5]))
```

```python
import functools
import math

import jax
import jax.numpy as jnp
from jax import lax
from jax.experimental import pallas as pl
from jax.experimental.pallas import tpu as pltpu

F32 = jnp.float32
BF16 = jnp.bfloat16
EPS = 1e-6
ROPE_THETA = 10000.0
LOG2E = math.log2(math.e)
NEG_INF = -1e30

LANES = 128
MXU_DIM = 256
VMEM_LIMIT = 56 * 1024 * 1024

TOKEN_TILE = 512
CONV_ROWS = 64
CONV_HALO = 32
PAGES_PER_CHUNK = 4
PAGED_GROUP = 4


def _rms(x, g):
    return x * lax.rsqrt(jnp.mean(x * x, axis=-1, keepdims=True) + EPS) * g


def _dot(a, b):
    return jnp.dot(a, b, preferred_element_type=F32)


def _dot_t(a, b):
    return lax.dot_general(a, b, (((1,), (1,)), ((), ())), preferred_element_type=F32)


def _seg_ms(x, s_ref):
    x2 = (x * x).astype(BF16)
    w = s_ref.shape[0]
    parts = [_dot(x2[:, j * w:(j + 1) * w], s_ref[...]) for j in range(x.shape[1] // w)]
    return parts[0] if len(parts) == 1 else jnp.concatenate(parts, axis=1)


def _in_body(x_ref, tabq_ref, tabk_ref, gmix_ref, win_ref, gql_ref, wuq_ref, gqv_ref, sq_ref,
             gkv_ref, wuk_ref, wuvt_ref, gkv_vec_ref, sk_ref, gkpe_ref, skpe_ref,
             q_ref, k_ref, vt_ref, ckv_ref, kpe_ref, glu_ref, *, ql, kl, cc, nh, rope):
    x = x_ref[0]
    h = _rms(x, gmix_ref[...]).astype(BF16)
    z = _dot(h, win_ref[...])
    o_a = ql + kl
    o_g = o_a + cc
    o_k = o_g + cc
    glu_ref[0] = z[:, o_a:o_g] * jax.nn.sigmoid(z[:, o_g:o_k])

    cqn = _rms(z[:, :ql], gql_ref[...]).astype(BF16)
    qa = _dot(cqn, wuq_ref[...])
    tq = jnp.concatenate([tabq_ref[...]] * nh, axis=1)
    q = qa * lax.rsqrt(_seg_ms(qa, sq_ref) + EPS) * gqv_ref[...] * tq
    q_ref[0] = q.astype(q_ref.dtype)

    ckvn = _rms(z[:, ql:o_a], gkv_ref[...])
    ckv_ref[0] = ckvn
    cb = ckvn.astype(BF16)
    vt_ref[0, 0] = _dot_t(wuvt_ref[...], cb).astype(vt_ref.dtype)
    ka = _dot(cb, wuk_ref[...])
    kn = ka * lax.rsqrt(_seg_ms(ka, sk_ref) + EPS) * gkv_vec_ref[...]

    kp = z[:, o_k:o_k + LANES]
    t = kp * lax.rsqrt(_seg_ms(kp, skpe_ref) + EPS) * gkpe_ref[...] * tabk_ref[...]
    kd = t + pltpu.roll(t, rope, axis=1)
    kpe_ref[0] = kd[:, :rope]
    lane = lax.broadcasted_iota(jnp.int32, kd.shape, 1)
    kdm = jnp.where(lane >= LANES - 2 * rope, kd, 0.0)
    k_ref[0] = (kn + jnp.concatenate([kdm] * nh, axis=1)).astype(k_ref.dtype)


def _in_proj(x, tabq, tabk, p, q_dtype):
    B, T, D = x.shape
    tm = min(TOKEN_TILE, T)
    nh, ql, kl, cc, rope, vw = p["nh"], p["ql"], p["kl"], p["cc"], p["rope"], p["vw"]
    const = lambda a: pl.BlockSpec(a.shape, lambda b, i: (0,) * a.ndim, pipeline_mode=pl.Buffered(1))
    tok = lambda w: pl.BlockSpec((1, tm, w), lambda b, i: (b, i, 0))
    tab = pl.BlockSpec((tm, LANES), lambda b, i: (i, 0))
    consts = [p["g_mix"], p["w_in"], p["g_ql"], p["w_uq"], p["g_qvec"], p["s_q"],
              p["g_kv"], p["w_uk_pad"], p["w_uvt"], p["g_kvec"], p["s_k"], p["g_kpe"], p["s_kpe"]]
    out_shape = (
        jax.ShapeDtypeStruct((B, T, nh * LANES), q_dtype),
        jax.ShapeDtypeStruct((B, T, nh * LANES), BF16),
        jax.ShapeDtypeStruct((B, T // tm, vw, tm), BF16),
        jax.ShapeDtypeStruct((B, T, kl), F32),
        jax.ShapeDtypeStruct((B, T, rope), F32),
        jax.ShapeDtypeStruct((B, T, cc), F32),
    )
    return pl.pallas_call(
        functools.partial(_in_body, ql=ql, kl=kl, cc=cc, nh=nh, rope=rope),
        grid=(B, T // tm),
        in_specs=[tok(D), tab, tab] + [const(a) for a in consts],
        out_specs=[tok(nh * LANES), tok(nh * LANES), pl.BlockSpec((1, 1, vw, tm), lambda b, i: (b, i, 0, 0)), tok(kl), tok(rope), tok(cc)],
        out_shape=out_shape,
        compiler_params=pltpu.CompilerParams(
            dimension_semantics=("parallel", "parallel"), vmem_limit_bytes=VMEM_LIMIT),
        name="in_proj",
    )(x, tabq, tabk, *consts)


def _flash_body(q_ref, k_ref, vt_ref, o_ref, m_ref, l_ref, acc_ref, *, tile, vh):
    i = pl.program_id(2)
    m_ref[...] = jnp.full(m_ref.shape, NEG_INF, F32)
    l_ref[...] = jnp.zeros(l_ref.shape, F32)
    acc_ref[...] = jnp.zeros(acc_ref.shape, F32)

    def step(j, masked):
        start = pl.multiple_of(j * tile, tile)
        for h in range(2):
            k = k_ref[0, pl.ds(start, tile), h * LANES:(h + 1) * LANES]
            st = _dot_t(k, q_ref[0, :, h * LANES:(h + 1) * LANES])
            if masked:
                key = lax.broadcasted_iota(jnp.int32, (tile, tile), 0)
                qry = lax.broadcasted_iota(jnp.int32, (tile, tile), 1)
                st = jnp.where(key <= qry, st, NEG_INF)
            m_prev = m_ref[h]
            m_new = jnp.maximum(m_prev, jnp.max(st, axis=0, keepdims=True))
            alpha = jnp.exp2(m_prev - m_new)
            pt = jnp.exp2(st - m_new)
            l_ref[h] = alpha * l_ref[h] + jnp.sum(pt, axis=0, keepdims=True)
            acc_ref[h] = alpha * acc_ref[h] + _dot(vt_ref[0, j, h * vh:(h + 1) * vh, :], pt.astype(BF16))
            m_ref[h] = m_new

    def full_step(j, carry):
        step(j, False)
        return carry

    lax.fori_loop(0, i, full_step, 0)
    step(i, True)
    o_t = jnp.concatenate([acc_ref[0] / l_ref[0], acc_ref[1] / l_ref[1]], axis=0)
    o_ref[0] = o_t.T


def _flash(q, k, vt, vh):
    B, T, W = q.shape
    nhp = W // (2 * LANES)
    tile = vt.shape[3]
    nt = T // tile
    return pl.pallas_call(
        functools.partial(_flash_body, tile=tile, vh=vh),
        grid=(B, nhp, nt),
        in_specs=[
            pl.BlockSpec((1, tile, 2 * LANES), lambda b, hp, i: (b, i, hp)),
            pl.BlockSpec((1, T, 2 * LANES), lambda b, hp, i: (b, 0, hp)),
            pl.BlockSpec((1, nt, 2 * vh, tile), lambda b, hp, i: (b, 0, hp, 0)),
        ],
        out_specs=pl.BlockSpec((1, tile, 2 * vh), lambda b, hp, i: (b, i, hp)),
        out_shape=jax.ShapeDtypeStruct((B, T, nhp * 2 * vh), F32),
        scratch_shapes=[pltpu.VMEM((2, 1, tile), F32), pltpu.VMEM((2, 1, tile), F32),
                        pltpu.VMEM((2, vh, tile), F32)],
        compiler_params=pltpu.CompilerParams(
            dimension_semantics=("parallel", "parallel", "arbitrary"), vmem_limit_bytes=VMEM_LIMIT),
        name="flash",
    )(q, k, vt)


def _absorb_body(q_ref, gk_ref, wukt_ref, pmat_ref, a_ref, qpe_ref, *, nh, kl):
    q = q_ref[...]
    qpe_ref[...] = _dot(q.astype(BF16), pmat_ref[...]).astype(qpe_ref.dtype)
    qg = (q * gk_ref[...]).astype(BF16)
    for h in range(nh):
        a_ref[:, h * kl:(h + 1) * kl] = _dot(qg[:, h * LANES:(h + 1) * LANES], wukt_ref[h]).astype(a_ref.dtype)


def _absorb(q, p):
    N, W = q.shape
    nh, kl, rope = p["nh"], p["kl"], p["rope"]
    tm = min(256, N)
    const = lambda a: pl.BlockSpec(a.shape, lambda i: (0,) * a.ndim, pipeline_mode=pl.Buffered(1))
    consts = [p["g_kvec"], p["w_ukt_pad"], p["p_mat"]]
    return pl.pallas_call(
        functools.partial(_absorb_body, nh=nh, kl=kl),
        grid=(N // tm,),
        in_specs=[pl.BlockSpec((tm, W), lambda i: (i, 0))] + [const(a) for a in consts],
        out_specs=[pl.BlockSpec((tm, nh * kl), lambda i: (i, 0)),
                   pl.BlockSpec((tm, nh * rope), lambda i: (i, 0))],
        out_shape=(jax.ShapeDtypeStruct((N, nh * kl), BF16), jax.ShapeDtypeStruct((N, nh * rope), BF16)),
        compiler_params=pltpu.CompilerParams(dimension_semantics=("parallel",), vmem_limit_bytes=VMEM_LIMIT),
        name="absorb",
    )(q, *consts)


def _paged_body(pt_ref, a_ref, qpe_ref, wukt_ref, cnew_ref, knewt_ref, cache_c_ref, cache_kt_ref, o_ref,
                cbuf, kbuf, sem, m_ref, l_ref, acc_ref, *, layer, n_pages, page, nh, nope):
    g = pl.program_id(0)
    ng = pl.num_programs(0)
    grp, rows, _ = a_ref.shape
    ch = PAGES_PER_CHUNK
    n_chunks = n_pages // ch

    def copies(gg, cc, slot):
        out = []
        for e in range(grp):
            for i in range(ch):
                pg = pt_ref[(gg * grp + e) * n_pages + cc * ch + i]
                out.append(pltpu.make_async_copy(cache_c_ref.at[layer, pg],
                                                 cbuf.at[slot, e, pl.ds(i * page, page)], sem.at[slot, 0]))
                out.append(pltpu.make_async_copy(cache_kt_ref.at[layer, pg],
                                                 kbuf.at[slot, e, :, pl.ds(i * page, page)], sem.at[slot, 1]))
        return out

    def fetch(gg, cc, slot):
        for c in copies(gg, cc, slot):
            c.start()

    @pl.when(g == 0)
    def _():
        fetch(0, 0, 0)

    m_ref[...] = jnp.full(m_ref.shape, NEG_INF, F32)
    l_ref[...] = jnp.zeros(l_ref.shape, F32)
    acc_ref[...] = jnp.zeros(acc_ref.shape, F32)

    def process(e, ckv, kpet, mask):
        cb = ckv.astype(BF16)
        kt = _dot_t(wukt_ref[...], cb)
        ss = jnp.sum((kt * kt).reshape(nope, nh, kt.shape[1]), axis=0)
        r = lax.rsqrt(ss * (1.0 / nope) + EPS)
        s = (_dot_t(a_ref[e], cb) * jnp.concatenate([r] * (rows // nh), axis=0)
             + _dot(qpe_ref[e], kpet.astype(BF16)))
        if mask is not None:
            s = jnp.where(mask, s, NEG_INF)
        m_prev = m_ref[e]
        m_new = jnp.maximum(m_prev, jnp.max(s, axis=-1, keepdims=True))
        alpha = jnp.exp2(m_prev - m_new)
        pr = jnp.exp2(s - m_new)
        l_ref[e] = alpha * l_ref[e] + jnp.sum(pr, axis=-1, keepdims=True)
        acc_ref[e] = alpha * acc_ref[e] + _dot(pr.astype(BF16), cb)
        m_ref[e] = m_new

    def chunk(c, carry):
        slot = c % 2

        @pl.when(c + 1 < n_chunks)
        def _():
            fetch(g, c + 1, 1 - slot)

        @pl.when(jnp.logical_and(c + 1 == n_chunks, g + 1 < ng))
        def _():
            fetch(g + 1, 0, 1 - slot)

        for cp in copies(g, c, slot):
            cp.wait()
        for e in range(grp):
            process(e, cbuf[slot, e], kbuf[slot, e], None)
        return carry

    lax.fori_loop(0, n_chunks, chunk, 0)

    tq = cnew_ref.shape[1]
    zpad = jnp.zeros((page - tq, cnew_ref.shape[2]), F32)
    qi = lax.broadcasted_iota(jnp.int32, (rows, page), 0) // nh
    kj = lax.broadcasted_iota(jnp.int32, (rows, page), 1)
    for e in range(grp):
        process(e, jnp.concatenate([cnew_ref[e], zpad], axis=0), knewt_ref[e], kj <= qi)
        o_ref[e] = acc_ref[e] / l_ref[e]


def _paged_attn(page_table, a, qpe, cnew, knewt, cache_c, cache_kt, p, layer):
    Bd, rows, kl = a.shape
    nh, rope, nope = p["nh"], p["rope"], p["nope"]
    n_pages = page_table.shape[1]
    page = cache_c.shape[2]
    grp = math.gcd(PAGED_GROUP, Bd)
    assert n_pages % (2 * PAGES_PER_CHUNK) == 0, "chunk slots alternate per grid step"
    keys = PAGES_PER_CHUNK * page
    tq = cnew.shape[1]
    row_blk = lambda r, w: pl.BlockSpec((grp, r, w), lambda b, pt: (b, 0, 0))
    grid_spec = pltpu.PrefetchScalarGridSpec(
        num_scalar_prefetch=1,
        grid=(Bd // grp,),
        in_specs=[
            row_blk(rows, kl),
            row_blk(rows, rope),
            pl.BlockSpec(p["w_ukt_perm"].shape, lambda b, pt: (0, 0), pipeline_mode=pl.Buffered(1)),
            row_blk(tq, kl),
            row_blk(rope, page),
            pl.BlockSpec(memory_space=pl.ANY),
            pl.BlockSpec(memory_space=pl.ANY),
        ],
        out_specs=row_blk(rows, kl),
        scratch_shapes=[
            pltpu.VMEM((2, grp, keys, kl), F32),
            pltpu.VMEM((2, grp, rope, keys), F32),
            pltpu.SemaphoreType.DMA((2, 2)),
            pltpu.VMEM((grp, rows, 1), F32),
            pltpu.VMEM((grp, rows, 1), F32),
            pltpu.VMEM((grp, rows, kl), F32),
        ],
    )
    return pl.pallas_call(
        functools.partial(_paged_body, layer=layer, n_pages=n_pages, page=page, nh=nh, nope=nope),
        grid_spec=grid_spec,
        out_shape=jax.ShapeDtypeStruct((Bd, rows, kl), F32),
        compiler_params=pltpu.CompilerParams(dimension_semantics=("arbitrary",), vmem_limit_bytes=VMEM_LIMIT),
        name="paged_attn",
    )(page_table.reshape(-1), a, qpe, p["w_ukt_perm"], cnew, knewt, cache_c, cache_kt)


def _sproj_body(o_ref, w_ref, out_ref):
    out_ref[...] = _dot(o_ref[...].astype(BF16), w_ref[...])


def _sproj(o_flat, w_bd):
    N, K = o_flat.shape
    tm = min(TOKEN_TILE, N)
    return pl.pallas_call(
        _sproj_body,
        grid=(N // tm,),
        in_specs=[pl.BlockSpec((tm, K), lambda i: (i, 0)),
                  pl.BlockSpec(w_bd.shape, lambda i: (0, 0), pipeline_mode=pl.Buffered(1))],
        out_specs=pl.BlockSpec((tm, w_bd.shape[1]), lambda i: (i, 0)),
        out_shape=jax.ShapeDtypeStruct((N, w_bd.shape[1]), F32),
        compiler_params=pltpu.CompilerParams(dimension_semantics=("parallel",), vmem_limit_bytes=VMEM_LIMIT),
        name="sproj",
    )(o_flat, w_bd)


def _post_tail(y, x, attn, pe, refs, out_ref):
    (lng_ref, lnb_ref, goc_ref, goa_ref, woa_ref, woc_ref, gffn_ref, wg_ref, wu_ref, wd_ref,
     gple_ref, wpg_ref, wpp_ref) = refs
    mu = jnp.mean(y, axis=-1, keepdims=True)
    yc = y - mu
    yn = yc * lax.rsqrt(jnp.mean(yc * yc, axis=-1, keepdims=True) + EPS) * lng_ref[...] + lnb_ref[...]
    conv = yn * jax.nn.sigmoid(yn)
    x = x + _dot(_rms(attn, goa_ref[...]).astype(BF16), woa_ref[...])
    x = x + _dot(_rms(conv, goc_ref[...]).astype(BF16), woc_ref[...])
    h = _rms(x, gffn_ref[...]).astype(BF16)
    g = _dot(h, wg_ref[...])
    u = _dot(h, wu_ref[...])
    x = x + _dot((g * jax.nn.sigmoid(g) * u).astype(BF16), wd_ref[...])
    gate = jax.nn.sigmoid(_dot(_rms(x, gple_ref[...]).astype(BF16), wpg_ref[...]))
    out_ref[0] = x + gate * _dot(pe.astype(BF16), wpp_ref[...])


def _post_prompt_body(x_ref, attn_ref, pe_ref, glu_ref, prev_ref, cw_ref, cb_ref, *rest, taps):
    refs, out_ref, xs_ref, y_ref = rest[:-3], rest[-3], rest[-2], rest[-1]
    i = pl.program_id(1)
    tm = glu_ref.shape[1]
    xs_ref[0:CONV_HALO, :] = jnp.where(i > 0, prev_ref[0], 0.0)
    xs_ref[CONV_HALO:, :] = glu_ref[0]
    off = CONV_HALO - (taps - 1)
    for r0 in range(0, tm, CONV_ROWS):
        acc = jnp.broadcast_to(cb_ref[...], (CONV_ROWS, cb_ref.shape[1]))
        for k in range(taps):
            acc = acc + cw_ref[k:k + 1, :] * xs_ref[r0 + off + k:r0 + off + k + CONV_ROWS, :]
        y_ref[r0:r0 + CONV_ROWS, :] = acc
    _post_tail(y_ref[...], x_ref[0], attn_ref[0], pe_ref[0], refs, out_ref)


def _post_sample_body(x_ref, attn_ref, pe_ref, xc_ref, cw_ref, cb_ref, *rest, taps):
    refs, out_ref = rest[:-1], rest[-1]
    nb, _, cc = xc_ref.shape
    tq = x_ref.shape[1] // nb
    acc = jnp.broadcast_to(cb_ref[...].reshape(1, 1, cc), (nb, tq, cc))
    for k in range(taps):
        acc = acc + cw_ref[k:k + 1, :].reshape(1, 1, cc) * xc_ref[:, k:k + tq, :]
    _post_tail(acc.reshape(nb * tq, cc), x_ref[0], attn_ref[0], pe_ref[0], refs, out_ref)


def _post(x, attn, pe, conv_in, p, sample):
    B, T, D = x.shape
    tm = min(TOKEN_TILE, T)
    cc, taps = p["cc"], p["taps"]
    const = lambda a: pl.BlockSpec(a.shape, lambda b, i: (0,) * a.ndim, pipeline_mode=pl.Buffered(1))
    tok = lambda w: pl.BlockSpec((1, tm, w), lambda b, i: (b, i, 0))
    consts = [p["conv_w"], p["conv_b"], p["ln_g"], p["ln_b"], p["g_oc"], p["g_oa"], p["w_oa"], p["w_oc"],
              p["g_ffn"], p["w_gate"], p["w_up"], p["w_down"], p["g_ple"], p["w_pg"], p["w_pp"]]
    if sample:
        tq = conv_in.shape[1] - (taps - 1)
        nb = tm // tq
        conv_specs = [pl.BlockSpec((nb,) + conv_in.shape[1:], lambda b, i: (i, 0, 0))]
        conv_args = [conv_in]
        body = functools.partial(_post_sample_body, taps=taps)
        scratch = []
    else:
        per = tm // CONV_HALO
        conv_specs = [tok(cc),
                      pl.BlockSpec((1, CONV_HALO, cc), lambda b, i: (b, jnp.maximum(i * per - 1, 0), 0))]
        conv_args = [conv_in, conv_in]
        body = functools.partial(_post_prompt_body, taps=taps)
        scratch = [pltpu.VMEM((tm + CONV_HALO, cc), F32), pltpu.VMEM((tm, cc), F32)]
    return pl.pallas_call(
        body,
        grid=(B, T // tm),
        in_specs=[tok(D), tok(attn.shape[2]), tok(pe.shape[2])] + conv_specs + [const(a) for a in consts],
        out_specs=tok(D),
        out_shape=jax.ShapeDtypeStruct((B, T, D), F32),
        scratch_shapes=scratch,
        compiler_params=pltpu.CompilerParams(
            dimension_semantics=("parallel", "parallel"), vmem_limit_bytes=VMEM_LIMIT),
        name="post_sample" if sample else "post_prompt",
    )(x, attn, pe, *conv_args, *consts)


def _rope_tables(pos, rope):
    inv = 1.0 / (ROPE_THETA ** (jnp.arange(0, rope, 2, dtype=F32) / rope))
    ang = pos.astype(F32)[:, None] * inv[None, :]
    cos, sin = jnp.cos(ang), jnp.sin(ang)
    cos2 = jnp.concatenate([cos, cos], axis=1)
    sin2 = jnp.concatenate([-sin, sin], axis=1)
    ones = jnp.ones((pos.shape[0], LANES - 2 * rope), F32)
    tabq = jnp.concatenate([ones, cos2, sin2], axis=1)
    tabk = jnp.concatenate([cos2, sin2, cos2, sin2], axis=1)
    return tabq, tabk


def _swap_halves(w, axis):
    a, b = jnp.split(w, 2, axis=axis)
    return jnp.concatenate([b, a], axis=axis)


def _layer_params(i, norm_mix, w_in, q_lora_norm, w_uq, kv_norm, w_uk, w_uv, q_nope_norm, k_nope_norm,
                  q_pe_norm, k_pe_norm, conv_w, conv_b, conv_ln_g, conv_ln_b, norm_out_attn, norm_out_conv,
                  w_out, norm_ffn, w_gate, w_up, w_down, norm_ple, w_ple_gate, w_ple_proj):
    ql, kl = q_lora_norm.shape[1], kv_norm.shape[1]
    nope, rope = q_nope_norm.shape[1], q_pe_norm.shape[1]
    cc, taps = conv_w.shape[2], conv_w.shape[1]
    nh = w_uk.shape[2] // nope
    vh = w_uv.shape[2] // nh
    assert nope + 2 * rope == LANES and 4 * rope == LANES and 2 * vh == LANES and nh % 2 == 0
    assert taps - 1 <= CONV_HALO
    scale = LOG2E / math.sqrt(nope + rope)
    row = lambda v: v.reshape(1, -1).astype(F32)

    wi = w_in[i]
    w_q, w_kv, w_kpe, w_a, w_g = jnp.split(wi, [ql, ql + kl, ql + kl + rope, ql + kl + rope + cc], axis=1)
    w_kpe_sw = _swap_halves(w_kpe, 1)
    w_in_cat = jnp.concatenate([w_q, w_kv, w_a, w_g, w_kpe, w_kpe_sw, w_kpe, w_kpe_sw], axis=1).astype(BF16)

    wq3 = w_uq[i].reshape(ql, nh, nope + rope)
    wq_pe = wq3[:, :, nope:]
    w_uq_cat = jnp.concatenate([wq3[:, :, :nope], wq_pe, _swap_halves(wq_pe, 2)], axis=2)
    w_uq_cat = w_uq_cat.reshape(ql, nh * LANES).astype(BF16)
    gq = jnp.concatenate([q_nope_norm[i], q_pe_norm[i], _swap_halves(q_pe_norm[i], 0)]) * scale
    g_qvec = row(jnp.tile(gq, nh))

    wk3 = w_uk[i].reshape(kl, nh, nope)
    w_uk_pad = jnp.concatenate([wk3, jnp.zeros((kl, nh, LANES - nope), F32)], axis=2).reshape(kl, nh * LANES)
    gk = jnp.concatenate([k_nope_norm[i], jnp.zeros((LANES - nope,), F32)])
    g_kvec = row(jnp.tile(gk, nh))
    g_kpe = row(jnp.tile(jnp.concatenate([k_pe_norm[i], _swap_halves(k_pe_norm[i], 0)]), 2))

    li = jnp.arange(LANES)
    blk_q = (jnp.where((li[:, None] < nope) & (li[None, :] < nope), 1.0 / nope, 0.0)
             + jnp.where((li[:, None] >= nope) & (li[:, None] < nope + rope) & (li[None, :] >= nope),
                         1.0 / rope, 0.0))
    blk_k = jnp.where((li[:, None] < nope) & (li[None, :] < nope), 1.0 / nope, 0.0)
    eye2 = jnp.eye(MXU_DIM // LANES, dtype=F32)
    s_q = jnp.kron(eye2, blk_q).astype(BF16)
    s_k = jnp.kron(eye2, blk_k).astype(BF16)
    s_kpe = jnp.where(li[:, None] < rope, 1.0 / rope, 0.0) * jnp.ones((1, LANES), F32)
    s_kpe = s_kpe.astype(BF16)

    w_ukt_pad = jnp.concatenate([jnp.transpose(wk3, (1, 2, 0)),
                                 jnp.zeros((nh, LANES - nope, kl), F32)], axis=1).astype(BF16)
    w_ukt_perm = jnp.transpose(wk3, (2, 1, 0)).reshape(nope * nh, kl).astype(BF16)
    src = jnp.arange(nh * LANES)
    lane_in = src % LANES
    dst = (src // LANES) * rope + (lane_in - nope) % rope
    p_mat = jnp.where((lane_in >= nope)[:, None] & (dst[:, None] == jnp.arange(nh * rope)[None, :]), 1.0, 0.0)
    p_mat = p_mat.astype(BF16)
    wv3 = w_uv[i].reshape(kl, nh, vh)
    w_bd = jnp.einsum("chv,hg->hcgv", wv3, jnp.eye(nh, dtype=F32)).reshape(nh * kl, nh * vh).astype(BF16)

    mw = norm_out_attn.shape[1]
    return dict(
        nh=nh, ql=ql, kl=kl, cc=cc, rope=rope, nope=nope, vh=vh, vw=nh * vh, taps=taps,
        g_mix=row(norm_mix[i]), w_in=w_in_cat, g_ql=row(q_lora_norm[i]), w_uq=w_uq_cat, g_qvec=g_qvec, s_q=s_q,
        g_kv=row(kv_norm[i]), w_uk_pad=w_uk_pad.astype(BF16), w_uvt=w_uv[i].T.astype(BF16), g_kvec=g_kvec, s_k=s_k, g_kpe=g_kpe, s_kpe=s_kpe,
        w_ukt_pad=w_ukt_pad, w_ukt_perm=w_ukt_perm, p_mat=p_mat, w_bd=w_bd,
        conv_w=conv_w[i].astype(F32), conv_b=row(conv_b[i]), ln_g=row(conv_ln_g[i]), ln_b=row(conv_ln_b[i]),
        g_oc=row(norm_out_conv[i]), g_oa=row(norm_out_attn[i]),
        w_oa=w_out[i][:mw].astype(BF16), w_oc=w_out[i][mw:].astype(BF16),
        g_ffn=row(norm_ffn[i]), w_gate=w_gate[i].astype(BF16), w_up=w_up[i].astype(BF16),
        w_down=w_down[i].astype(BF16), g_ple=row(norm_ple[i]), w_pg=w_ple_gate[i].astype(BF16),
        w_pp=w_ple_proj[i].astype(BF16),
    )


def kernel(x_prompt, x_sample, cache_ckv, cache_kpe, state_conv, page_table, p_prompt, p_sample, norm_mix, w_in, q_lora_norm, w_uq, kv_norm, w_uk, w_uv, q_nope_norm, k_nope_norm, q_pe_norm, k_pe_norm, conv_w, conv_b, conv_ln_g, conv_ln_b, norm_out_attn, norm_out_conv, w_out, norm_ffn, w_gate, w_up, w_down, norm_ple, w_ple_gate, w_ple_proj):
    B, T, D = x_prompt.shape
    Bd, Tq, _ = x_sample.shape
    depth = w_in.shape[0]
    past_len = page_table.shape[1] * cache_ckv.shape[2]
    rope = q_pe_norm.shape[1]
    taps = conv_w.shape[1]
    tabq_p, tabk_p = _rope_tables(jnp.arange(T), rope)
    tabq_s, tabk_s = _rope_tables(past_len + jnp.arange(Tq), rope)
    tabq_s, tabk_s = jnp.tile(tabq_s, (Bd, 1)), jnp.tile(tabk_s, (Bd, 1))

    xp, xs = x_prompt, x_sample.reshape(1, Bd * Tq, D)
    outs = [[] for _ in range(6)]
    for i in range(depth):
        p = _layer_params(i, norm_mix, w_in, q_lora_norm, w_uq, kv_norm, w_uk, w_uv, q_nope_norm, k_nope_norm,
                          q_pe_norm, k_pe_norm, conv_w, conv_b, conv_ln_g, conv_ln_b, norm_out_attn,
                          norm_out_conv, w_out, norm_ffn, w_gate, w_up, w_down, norm_ple, w_ple_gate, w_ple_proj)
        nh, kl, cc = p["nh"], p["kl"], p["cc"]
        q, k, vt, ckv, kpe, glu = _in_proj(xp, tabq_p, tabk_p, p, BF16)
        attn = _flash(q, k, vt, p["vh"])
        xp = _post(xp, attn, p_prompt[i], glu, p, sample=False)
        outs[0].append(ckv)
        outs[1].append(kpe)
        outs[2].append(glu[:, T - (taps - 1):, :])
        q_s, _, _, ckv_s, kpe_s, glu_s = _in_proj(xs, tabq_s, tabk_s, p, F32)
        a_s, qpe_s = _absorb(q_s[0], p)
        ckv_s = ckv_s.reshape(Bd, Tq, kl)
        kpe_s = kpe_s.reshape(Bd, Tq, rope)
        page = cache_kpe.shape[2]
        knewt = jnp.pad(jnp.swapaxes(kpe_s, 1, 2), ((0, 0), (0, 0), (0, page - Tq)))
        o_lat = _paged_attn(page_table, a_s.reshape(Bd, Tq * nh, kl), qpe_s.reshape(Bd, Tq * nh, rope),
                            ckv_s, knewt, cache_ckv, jnp.swapaxes(cache_kpe, 2, 3), p, i)
        attn_s = _sproj(o_lat.reshape(Bd * Tq, nh * kl), p["w_bd"])
        xc = jnp.concatenate([state_conv[i], glu_s.reshape(Bd, Tq, cc)], axis=1)
        xs = _post(xs, attn_s[None], p_sample[i].reshape(1, Bd * Tq, -1), xc, p, sample=True)
        outs[3].append(ckv_s)
        outs[4].append(kpe_s)
        outs[5].append(xc[:, Tq:, :])
    return (xp, xs.reshape(Bd, Tq, D), jnp.stack(outs[0]), jnp.stack(outs[1]), jnp.stack(outs[2]),
            jnp.stack(outs[3]), jnp.stack(outs[4]), jnp.stack(outs[5]))
```

```python
import functools
import math

import jax
import jax.numpy as jnp
from jax import lax
from jax.experimental import pallas as pl
from jax.experimental.pallas import tpu as pltpu

F32 = jnp.float32
BF16 = jnp.bfloat16
EPS = 1e-6
ROPE_THETA = 10000.0
LOG2E = math.log2(math.e)
NEG_INF = -1e30

LANES = 128
MXU_DIM = 256
BF16_ROWS = 16
VMEM_LIMIT = 56 * 1024 * 1024
TOKEN_TILE = 512
CONV_ROWS = 64
CONV_HALO = 32
PAGES_PER_CHUNK = 4
PAGED_GROUP = 4


def _rms(x, g):
    return x * lax.rsqrt(jnp.mean(x * x, axis=-1, keepdims=True) + EPS) * g


def _dot(a, b):
    return jnp.dot(a, b, preferred_element_type=F32)


def _dot_t(a, b):
    return lax.dot_general(a, b, (((1,), (1,)), ((), ())), preferred_element_type=F32)


def _seg_ms(x, s_ref):
    x2 = (x * x).astype(BF16)
    w = s_ref.shape[0]
    parts = [_dot(x2[:, j * w:(j + 1) * w], s_ref[...]) for j in range(x.shape[1] // w)]
    return parts[0] if len(parts) == 1 else jnp.concatenate(parts, axis=1)


def _in_body(x_ref, tabq_ref, tabk_ref, gmix_ref, win_ref, gql_ref, wuq_ref, gqv_ref, sq_ref,
             gkv_ref, wuk_ref, wuvt_ref, gkv_vec_ref, sk_ref, gkpe_ref, skpe_ref,
             q_ref, k_ref, vt_ref, ckv_ref, kpe_ref, glu_ref, *, ql, kl, cc, nh, rope):
    x = x_ref[0]
    h = _rms(x, gmix_ref[...]).astype(BF16)
    z = _dot(h, win_ref[...])
    o_a = ql + kl
    o_g = o_a + cc
    o_k = o_g + cc
    glu_ref[0] = z[:, o_a:o_g] * jax.nn.sigmoid(z[:, o_g:o_k])

    cqn = _rms(z[:, :ql], gql_ref[...]).astype(BF16)
    qa = _dot(cqn, wuq_ref[...])
    tq = jnp.concatenate([tabq_ref[...]] * nh, axis=1)
    q = qa * lax.rsqrt(_seg_ms(qa, sq_ref) + EPS) * gqv_ref[...] * tq
    q_ref[0] = q.astype(q_ref.dtype)

    ckvn = _rms(z[:, ql:o_a], gkv_ref[...])
    ckv_ref[0] = ckvn
    cb = ckvn.astype(BF16)
    vt_ref[0, 0] = _dot_t(wuvt_ref[...], cb).astype(vt_ref.dtype)
    ka = _dot(cb, wuk_ref[...])
    kn = ka * lax.rsqrt(_seg_ms(ka, sk_ref) + EPS) * gkv_vec_ref[...]

    kp = z[:, o_k:o_k + LANES]
    t = kp * lax.rsqrt(_seg_ms(kp, skpe_ref) + EPS) * gkpe_ref[...] * tabk_ref[...]
    kd = t + pltpu.roll(t, rope, axis=1)
    kpe_ref[0] = kd[:, :rope]
    lane = lax.broadcasted_iota(jnp.int32, kd.shape, 1)
    kdm = jnp.where(lane >= LANES - 2 * rope, kd, 0.0)
    k_ref[0] = (kn + jnp.concatenate([kdm] * nh, axis=1)).astype(k_ref.dtype)


def _in_proj(x, tabq, tabk, p, q_dtype):
    B, T, D = x.shape
    tm = min(TOKEN_TILE, T)
    nh, ql, kl, cc, rope, vw = p["nh"], p["ql"], p["kl"], p["cc"], p["rope"], p["vw"]
    const = lambda a: pl.BlockSpec(a.shape, lambda b, i: (0,) * a.ndim, pipeline_mode=pl.Buffered(1))
    tok = lambda w: pl.BlockSpec((1, tm, w), lambda b, i: (b, i, 0))
    tab = pl.BlockSpec((tm, LANES), lambda b, i: (i, 0))
    consts = [p["g_mix"], p["w_in"], p["g_ql"], p["w_uq"], p["g_qvec"], p["s_q"],
              p["g_kv"], p["w_uk_pad"], p["w_uvt"], p["g_kvec"], p["s_k"], p["g_kpe"], p["s_kpe"]]
    out_shape = (
        jax.ShapeDtypeStruct((B, T, nh * LANES), q_dtype),
        jax.ShapeDtypeStruct((B, T, nh * LANES), BF16),
        jax.ShapeDtypeStruct((B, T // tm, vw, tm), BF16),
        jax.ShapeDtypeStruct((B, T, kl), F32),
        jax.ShapeDtypeStruct((B, T, rope), F32),
        jax.ShapeDtypeStruct((B, T, cc), F32),
    )
    return pl.pallas_call(
        functools.partial(_in_body, ql=ql, kl=kl, cc=cc, nh=nh, rope=rope),
        grid=(B, T // tm),
        in_specs=[tok(D), tab, tab] + [const(a) for a in consts],
        out_specs=[tok(nh * LANES), tok(nh * LANES), pl.BlockSpec((1, 1, vw, tm), lambda b, i: (b, i, 0, 0)), tok(kl), tok(rope), tok(cc)],
        out_shape=out_shape,
        compiler_params=pltpu.CompilerParams(
            dimension_semantics=("parallel", "parallel"), vmem_limit_bytes=VMEM_LIMIT),
        name="in_proj",
    )(x, tabq, tabk, *consts)


def _flash_body(q_ref, k_ref, vt_ref, o_ref, s_ref, mc_ref, m_ref, acc_ref, *, tile, vh):
    i = pl.program_id(2)
    ns = tile // MXU_DIM
    m_ref[...] = jnp.full(m_ref.shape, NEG_INF, F32)
    acc_ref[...] = jnp.zeros(acc_ref.shape, F32)
    ones = jnp.ones((acc_ref.shape[1] - vh, tile), BF16)

    def scores(h, j, s):
        sl = slice(s * MXU_DIM, (s + 1) * MXU_DIM)
        k = k_ref[0, pl.ds(pl.multiple_of(j * tile, tile), tile), h * LANES:(h + 1) * LANES]
        st = _dot_t(k, q_ref[0, sl, h * LANES:(h + 1) * LANES])
        s_ref[h, :, sl] = st
        mc_ref[h, :, sl] = jnp.max(st, axis=0, keepdims=True)

    def softmax_pv(h, j, s, masked):
        sl = slice(s * MXU_DIM, (s + 1) * MXU_DIM)
        st = s_ref[h, :, sl]
        if masked:
            key = lax.broadcasted_iota(jnp.int32, st.shape, 0)
            qry = lax.broadcasted_iota(jnp.int32, st.shape, 1) + s * MXU_DIM
            st = jnp.where(key <= qry, st, NEG_INF)
            m_cur = jnp.max(st, axis=0, keepdims=True)
        else:
            m_cur = mc_ref[h, :, sl]
        m_prev = m_ref[h, :, sl]
        m_new = jnp.maximum(m_prev, m_cur)
        alpha = jnp.exp2(m_prev - m_new)
        pt = jnp.exp2((st - m_new).astype(BF16))
        vt1 = jnp.concatenate([vt_ref[0, j, h * vh:(h + 1) * vh, :], ones], axis=0)
        acc_ref[h, :, sl] = alpha * acc_ref[h, :, sl] + _dot(vt1, pt)
        m_ref[h, :, sl] = m_new

    for s in range(ns):
        scores(0, 0, s)

    def block(j, carry):
        for s in range(ns):
            scores(1, j, s)
        for s in range(ns):
            softmax_pv(0, j, s, False)
        for s in range(ns):
            scores(0, j + 1, s)
        for s in range(ns):
            softmax_pv(1, j, s, False)
        return carry

    lax.fori_loop(0, i, block, 0)
    for s in range(ns):
        scores(1, i, s)
        softmax_pv(0, i, s, True)
    for s in range(ns):
        softmax_pv(1, i, s, True)
    o_t = jnp.concatenate([acc_ref[h, :vh, :] / acc_ref[h, vh:vh + 1, :] for h in range(2)], axis=0)
    o_ref[0] = o_t.T


def _flash(q, k, vt, vh):
    B, T, W = q.shape
    nhp = W // (2 * LANES)
    tile = vt.shape[3]
    nt = T // tile
    return pl.pallas_call(
        functools.partial(_flash_body, tile=tile, vh=vh),
        grid=(B, nhp, nt),
        in_specs=[
            pl.BlockSpec((1, tile, 2 * LANES), lambda b, hp, i: (b, i, hp)),
            pl.BlockSpec((1, T, 2 * LANES), lambda b, hp, i: (b, 0, hp)),
            pl.BlockSpec((1, nt, 2 * vh, tile), lambda b, hp, i: (b, 0, hp, 0)),
        ],
        out_specs=pl.BlockSpec((1, tile, 2 * vh), lambda b, hp, i: (b, i, hp)),
        out_shape=jax.ShapeDtypeStruct((B, T, nhp * 2 * vh), F32),
        scratch_shapes=[pltpu.VMEM((2, tile, tile), F32), pltpu.VMEM((2, 1, tile), F32),
                        pltpu.VMEM((2, 1, tile), F32), pltpu.VMEM((2, vh + BF16_ROWS, tile), F32)],
        compiler_params=pltpu.CompilerParams(
            dimension_semantics=("parallel", "parallel", "arbitrary"), vmem_limit_bytes=VMEM_LIMIT),
        name="flash",
    )(q, k, vt)


def _absorb_body(q_ref, gk_ref, wukt_ref, pmat_ref, a_ref, qpe_ref, *, nh, kl):
    q = q_ref[...]
    qpe_ref[...] = _dot(q.astype(BF16), pmat_ref[...]).astype(qpe_ref.dtype)
    qg = (q * gk_ref[...]).astype(BF16)
    for h in range(nh):
        a_ref[:, h * kl:(h + 1) * kl] = _dot(qg[:, h * LANES:(h + 1) * LANES], wukt_ref[h]).astype(a_ref.dtype)


def _absorb(q, p):
    N, W = q.shape
    nh, kl, rope = p["nh"], p["kl"], p["rope"]
    tm = min(256, N)
    const = lambda a: pl.BlockSpec(a.shape, lambda i: (0,) * a.ndim, pipeline_mode=pl.Buffered(1))
    consts = [p["g_kvec"], p["w_ukt_pad"], p["p_mat"]]
    return pl.pallas_call(
        functools.partial(_absorb_body, nh=nh, kl=kl),
        grid=(N // tm,),
        in_specs=[pl.BlockSpec((tm, W), lambda i: (i, 0))] + [const(a) for a in consts],
        out_specs=[pl.BlockSpec((tm, nh * kl), lambda i: (i, 0)),
                   pl.BlockSpec((tm, nh * rope), lambda i: (i, 0))],
        out_shape=(jax.ShapeDtypeStruct((N, nh * kl), BF16), jax.ShapeDtypeStruct((N, nh * rope), BF16)),
        compiler_params=pltpu.CompilerParams(dimension_semantics=("parallel",), vmem_limit_bytes=VMEM_LIMIT),
        name="absorb",
    )(q, *consts)


def _paged_body(pt_ref, a_ref, qpe_ref, wukt_ref, cnew_ref, knewt_ref, cache_c_ref, cache_kt_ref, o_ref,
                cbuf, kbuf, sem, s_ref, cb_ref, m_ref, l_ref, acc_ref, *, layer, n_pages, page, nh, nope):
    g = pl.program_id(0)
    ng = pl.num_programs(0)
    grp, rows, _ = a_ref.shape
    ch = PAGES_PER_CHUNK
    n_chunks = n_pages // ch

    def copies(gg, cc, slot):
        out = []
        for e in range(grp):
            for i in range(ch):
                pg = pt_ref[(gg * grp + e) * n_pages + cc * ch + i]
                out.append(pltpu.make_async_copy(cache_c_ref.at[layer, pg],
                                                 cbuf.at[slot, e, pl.ds(i * page, page)], sem.at[slot, 0]))
                out.append(pltpu.make_async_copy(cache_kt_ref.at[layer, pg],
                                                 kbuf.at[slot, e, :, pl.ds(i * page, page)], sem.at[slot, 1]))
        return out

    def fetch(gg, cc, slot):
        for c in copies(gg, cc, slot):
            c.start()

    @pl.when(g == 0)
    def _():
        fetch(0, 0, 0)

    m_ref[...] = jnp.full(m_ref.shape, NEG_INF, F32)
    l_ref[...] = jnp.zeros(l_ref.shape, F32)
    acc_ref[...] = jnp.zeros(acc_ref.shape, F32)

    def scores(e, ckv, kpet):
        cb = ckv.astype(BF16)
        kt = _dot_t(wukt_ref[...], cb)
        ss = jnp.sum((kt * kt).reshape(nope, nh, kt.shape[1]), axis=0)
        r = lax.rsqrt(ss * (1.0 / nope) + EPS)
        s = (_dot_t(a_ref[e], cb) * jnp.concatenate([r] * (rows // nh), axis=0)
             + _dot(qpe_ref[e], kpet.astype(BF16)))
        return s, cb

    def softmax_pv(e, s, cb):
        m_prev = m_ref[e]
        m_new = jnp.maximum(m_prev, jnp.max(s, axis=-1, keepdims=True))
        alpha = jnp.exp2(m_prev - m_new)
        pr = jnp.exp2(s - m_new)
        l_ref[e] = alpha * l_ref[e] + jnp.sum(pr, axis=-1, keepdims=True)
        acc_ref[e] = alpha * acc_ref[e] + _dot(pr.astype(BF16), cb)
        m_ref[e] = m_new

    def stage_scores(e, slot):
        s, cb = scores(e, cbuf[slot, e], kbuf[slot, e])
        s_ref[e] = s
        cb_ref[e] = cb

    def stage_softmax(e):
        softmax_pv(e, s_ref[e], cb_ref[e])

    def chunk(c, first):
        slot = c % 2

        @pl.when(c + 1 < n_chunks)
        def _():
            fetch(g, c + 1, 1 - slot)

        @pl.when(jnp.logical_and(c + 1 == n_chunks, g + 1 < ng))
        def _():
            fetch(g + 1, 0, 1 - slot)

        for cp in copies(g, c, slot):
            cp.wait()
        for e in range(grp):
            stage_scores(e, slot)
            if e > 0 or not first:
                stage_softmax((e - 1) % grp)

    chunk(0, True)

    def later_chunk(c, carry):
        chunk(c, False)
        return carry

    lax.fori_loop(1, n_chunks, later_chunk, 0)
    stage_softmax(grp - 1)

    tq = cnew_ref.shape[1]
    zpad = jnp.zeros((page - tq, cnew_ref.shape[2]), F32)
    qi = lax.broadcasted_iota(jnp.int32, (rows, page), 0) // nh
    kj = lax.broadcasted_iota(jnp.int32, (rows, page), 1)
    for e in range(grp):
        s, cb = scores(e, jnp.concatenate([cnew_ref[e], zpad], axis=0), knewt_ref[e])
        softmax_pv(e, jnp.where(kj <= qi, s, NEG_INF), cb)
        o_ref[e] = acc_ref[e] / l_ref[e]


def _paged_attn(page_table, a, qpe, cnew, knewt, cache_c, cache_kt, p, layer):
    Bd, rows, kl = a.shape
    nh, rope, nope = p["nh"], p["rope"], p["nope"]
    n_pages = page_table.shape[1]
    page = cache_c.shape[2]
    grp = math.gcd(PAGED_GROUP, Bd)
    assert n_pages % (2 * PAGES_PER_CHUNK) == 0, "chunk slots alternate per grid step"
    keys = PAGES_PER_CHUNK * page
    tq = cnew.shape[1]
    row_blk = lambda r, w: pl.BlockSpec((grp, r, w), lambda b, pt: (b, 0, 0))
    grid_spec = pltpu.PrefetchScalarGridSpec(
        num_scalar_prefetch=1,
        grid=(Bd // grp,),
        in_specs=[
            row_blk(rows, kl),
            row_blk(rows, rope),
            pl.BlockSpec(p["w_ukt_perm"].shape, lambda b, pt: (0, 0), pipeline_mode=pl.Buffered(1)),
            row_blk(tq, kl),
            row_blk(rope, page),
            pl.BlockSpec(memory_space=pl.ANY),
            pl.BlockSpec(memory_space=pl.ANY),
        ],
        out_specs=row_blk(rows, kl),
        scratch_shapes=[
            pltpu.VMEM((2, grp, keys, kl), F32),
            pltpu.VMEM((2, grp, rope, keys), F32),
            pltpu.SemaphoreType.DMA((2, 2)),
            pltpu.VMEM((grp, rows, keys), F32),
            pltpu.VMEM((grp, keys, kl), BF16),
            pltpu.VMEM((grp, rows, 1), F32),
            pltpu.VMEM((grp, rows, 1), F32),
            pltpu.VMEM((grp, rows, kl), F32),
        ],
    )
    return pl.pallas_call(
        functools.partial(_paged_body, layer=layer, n_pages=n_pages, page=page, nh=nh, nope=nope),
        grid_spec=grid_spec,
        out_shape=jax.ShapeDtypeStruct((Bd, rows, kl), F32),
        compiler_params=pltpu.CompilerParams(dimension_semantics=("arbitrary",), vmem_limit_bytes=VMEM_LIMIT),
        name="paged_attn",
    )(page_table.reshape(-1), a, qpe, p["w_ukt_perm"], cnew, knewt, cache_c, cache_kt)


def _sproj_body(o_ref, w_ref, out_ref):
    out_ref[...] = _dot(o_ref[...].astype(BF16), w_ref[...])


def _sproj(o_flat, w_bd):
    N, K = o_flat.shape
    tm = min(TOKEN_TILE, N)
    return pl.pallas_call(
        _sproj_body,
        grid=(N // tm,),
        in_specs=[pl.BlockSpec((tm, K), lambda i: (i, 0)),
                  pl.BlockSpec(w_bd.shape, lambda i: (0, 0), pipeline_mode=pl.Buffered(1))],
        out_specs=pl.BlockSpec((tm, w_bd.shape[1]), lambda i: (i, 0)),
        out_shape=jax.ShapeDtypeStruct((N, w_bd.shape[1]), F32),
        compiler_params=pltpu.CompilerParams(dimension_semantics=("parallel",), vmem_limit_bytes=VMEM_LIMIT),
        name="sproj",
    )(o_flat, w_bd)


def _post_tail(y, x, attn, pe, refs, out_ref):
    (lng_ref, lnb_ref, goc_ref, goa_ref, woa_ref, woc_ref, gffn_ref, wg_ref, wu_ref, wd_ref,
     gple_ref, wpg_ref, wpp_ref) = refs
    mu = jnp.mean(y, axis=-1, keepdims=True)
    yc = y - mu
    yn = yc * lax.rsqrt(jnp.mean(yc * yc, axis=-1, keepdims=True) + EPS) * lng_ref[...] + lnb_ref[...]
    conv = yn * jax.nn.sigmoid(yn)
    x = x + _dot(_rms(attn, goa_ref[...]).astype(BF16), woa_ref[...])
    x = x + _dot(_rms(conv, goc_ref[...]).astype(BF16), woc_ref[...])
    h = _rms(x, gffn_ref[...]).astype(BF16)
    g = _dot(h, wg_ref[...])
    u = _dot(h, wu_ref[...])
    x = x + _dot((g * jax.nn.sigmoid(g) * u).astype(BF16), wd_ref[...])
    gate = jax.nn.sigmoid(_dot(_rms(x, gple_ref[...]).astype(BF16), wpg_ref[...]))
    out_ref[0] = x + gate * _dot(pe.astype(BF16), wpp_ref[...])


def _post_prompt_body(x_ref, attn_ref, pe_ref, glu_ref, prev_ref, cw_ref, cb_ref, *rest, taps):
    refs, out_ref, xs_ref, y_ref = rest[:-3], rest[-3], rest[-2], rest[-1]
    i = pl.program_id(1)
    tm = glu_ref.shape[1]
    xs_ref[0:CONV_HALO, :] = jnp.where(i > 0, prev_ref[0], 0.0)
    xs_ref[CONV_HALO:, :] = glu_ref[0]
    off = CONV_HALO - (taps - 1)
    for r0 in range(0, tm, CONV_ROWS):
        acc = jnp.broadcast_to(cb_ref[...], (CONV_ROWS, cb_ref.shape[1]))
        for k in range(taps):
            acc = acc + cw_ref[k:k + 1, :] * xs_ref[r0 + off + k:r0 + off + k + CONV_ROWS, :]
        y_ref[r0:r0 + CONV_ROWS, :] = acc
    _post_tail(y_ref[...], x_ref[0], attn_ref[0], pe_ref[0], refs, out_ref)


def _post_sample_body(x_ref, attn_ref, pe_ref, xc_ref, cw_ref, cb_ref, *rest, taps):
    refs, out_ref = rest[:-1], rest[-1]
    nb, _, cc = xc_ref.shape
    tq = x_ref.shape[1] // nb
    acc = jnp.broadcast_to(cb_ref[...].reshape(1, 1, cc), (nb, tq, cc))
    for k in range(taps):
        acc = acc + cw_ref[k:k + 1, :].reshape(1, 1, cc) * xc_ref[:, k:k + tq, :]
    _post_tail(acc.reshape(nb * tq, cc), x_ref[0], attn_ref[0], pe_ref[0], refs, out_ref)


def _post(x, attn, pe, conv_in, p, sample):
    B, T, D = x.shape
    tm = min(TOKEN_TILE, T)
    cc, taps = p["cc"], p["taps"]
    const = lambda a: pl.BlockSpec(a.shape, lambda b, i: (0,) * a.ndim, pipeline_mode=pl.Buffered(1))
    tok = lambda w: pl.BlockSpec((1, tm, w), lambda b, i: (b, i, 0))
    consts = [p["conv_w"], p["conv_b"], p["ln_g"], p["ln_b"], p["g_oc"], p["g_oa"], p["w_oa"], p["w_oc"],
              p["g_ffn"], p["w_gate"], p["w_up"], p["w_down"], p["g_ple"], p["w_pg"], p["w_pp"]]
    if sample:
        tq = conv_in.shape[1] - (taps - 1)
        nb = tm // tq
        conv_specs = [pl.BlockSpec((nb,) + conv_in.shape[1:], lambda b, i: (i, 0, 0))]
        conv_args = [conv_in]
        body = functools.partial(_post_sample_body, taps=taps)
        scratch = []
    else:
        per = tm // CONV_HALO
        conv_specs = [tok(cc),
                      pl.BlockSpec((1, CONV_HALO, cc), lambda b, i: (b, jnp.maximum(i * per - 1, 0), 0))]
        conv_args = [conv_in, conv_in]
        body = functools.partial(_post_prompt_body, taps=taps)
        scratch = [pltpu.VMEM((tm + CONV_HALO, cc), F32), pltpu.VMEM((tm, cc), F32)]
    return pl.pallas_call(
        body,
        grid=(B, T // tm),
        in_specs=[tok(D), tok(attn.shape[2]), tok(pe.shape[2])] + conv_specs + [const(a) for a in consts],
        out_specs=tok(D),
        out_shape=jax.ShapeDtypeStruct((B, T, D), F32),
        scratch_shapes=scratch,
        compiler_params=pltpu.CompilerParams(
            dimension_semantics=("parallel", "parallel"), vmem_limit_bytes=VMEM_LIMIT),
        name="post_sample" if sample else "post_prompt",
    )(x, attn, pe, *conv_args, *consts)


def _rope_tables(pos, rope):
    inv = 1.0 / (ROPE_THETA ** (jnp.arange(0, rope, 2, dtype=F32) / rope))
    ang = pos.astype(F32)[:, None] * inv[None, :]
    cos, sin = jnp.cos(ang), jnp.sin(ang)
    cos2 = jnp.concatenate([cos, cos], axis=1)
    sin2 = jnp.concatenate([-sin, sin], axis=1)
    ones = jnp.ones((pos.shape[0], LANES - 2 * rope), F32)
    tabq = jnp.concatenate([ones, cos2, sin2], axis=1)
    tabk = jnp.concatenate([cos2, sin2, cos2, sin2], axis=1)
    return tabq, tabk


def _swap_halves(w, axis):
    a, b = jnp.split(w, 2, axis=axis)
    return jnp.concatenate([b, a], axis=axis)


def _layer_params(i, norm_mix, w_in, q_lora_norm, w_uq, kv_norm, w_uk, w_uv, q_nope_norm, k_nope_norm,
                  q_pe_norm, k_pe_norm, conv_w, conv_b, conv_ln_g, conv_ln_b, norm_out_attn, norm_out_conv,
                  w_out, norm_ffn, w_gate, w_up, w_down, norm_ple, w_ple_gate, w_ple_proj):
    ql, kl = q_lora_norm.shape[1], kv_norm.shape[1]
    nope, rope = q_nope_norm.shape[1], q_pe_norm.shape[1]
    cc, taps = conv_w.shape[2], conv_w.shape[1]
    nh = w_uk.shape[2] // nope
    vh = w_uv.shape[2] // nh
    assert nope + 2 * rope == LANES and 4 * rope == LANES and 2 * vh == LANES and nh % 2 == 0
    assert taps - 1 <= CONV_HALO
    scale = LOG2E / math.sqrt(nope + rope)
    row = lambda v: v.reshape(1, -1).astype(F32)

    wi = w_in[i]
    w_q, w_kv, w_kpe, w_a, w_g = jnp.split(wi, [ql, ql + kl, ql + kl + rope, ql + kl + rope + cc], axis=1)
    w_kpe_sw = _swap_halves(w_kpe, 1)
    w_in_cat = jnp.concatenate([w_q, w_kv, w_a, w_g, w_kpe, w_kpe_sw, w_kpe, w_kpe_sw], axis=1).astype(BF16)

    wq3 = w_uq[i].reshape(ql, nh, nope + rope)
    wq_pe = wq3[:, :, nope:]
    w_uq_cat = jnp.concatenate([wq3[:, :, :nope], wq_pe, _swap_halves(wq_pe, 2)], axis=2)
    w_uq_cat = w_uq_cat.reshape(ql, nh * LANES).astype(BF16)
    gq = jnp.concatenate([q_nope_norm[i], q_pe_norm[i], _swap_halves(q_pe_norm[i], 0)]) * scale
    g_qvec = row(jnp.tile(gq, nh))

    wk3 = w_uk[i].reshape(kl, nh, nope)
    w_uk_pad = jnp.concatenate([wk3, jnp.zeros((kl, nh, LANES - nope), F32)], axis=2).reshape(kl, nh * LANES)
    gk = jnp.concatenate([k_nope_norm[i], jnp.zeros((LANES - nope,), F32)])
    g_kvec = row(jnp.tile(gk, nh))
    g_kpe = row(jnp.tile(jnp.concatenate([k_pe_norm[i], _swap_halves(k_pe_norm[i], 0)]), 2))

    li = jnp.arange(LANES)
    blk_q = (jnp.where((li[:, None] < nope) & (li[None, :] < nope), 1.0 / nope, 0.0)
             + jnp.where((li[:, None] >= nope) & (li[:, None] < nope + rope) & (li[None, :] >= nope),
                         1.0 / rope, 0.0))
    blk_k = jnp.where((li[:, None] < nope) & (li[None, :] < nope), 1.0 / nope, 0.0)
    eye2 = jnp.eye(MXU_DIM // LANES, dtype=F32)
    s_q = jnp.kron(eye2, blk_q).astype(BF16)
    s_k = jnp.kron(eye2, blk_k).astype(BF16)
    s_kpe = jnp.where(li[:, None] < rope, 1.0 / rope, 0.0) * jnp.ones((1, LANES), F32)
    s_kpe = s_kpe.astype(BF16)

    w_ukt_pad = jnp.concatenate([jnp.transpose(wk3, (1, 2, 0)),
                                 jnp.zeros((nh, LANES - nope, kl), F32)], axis=1).astype(BF16)
    w_ukt_perm = jnp.transpose(wk3, (2, 1, 0)).reshape(nope * nh, kl).astype(BF16)
    src = jnp.arange(nh * LANES)
    lane_in = src % LANES
    dst = (src // LANES) * rope + (lane_in - nope) % rope
    p_mat = jnp.where((lane_in >= nope)[:, None] & (dst[:, None] == jnp.arange(nh * rope)[None, :]), 1.0, 0.0)
    p_mat = p_mat.astype(BF16)
    wv3 = w_uv[i].reshape(kl, nh, vh)
    w_bd = jnp.einsum("chv,hg->hcgv", wv3, jnp.eye(nh, dtype=F32)).reshape(nh * kl, nh * vh).astype(BF16)

    mw = norm_out_attn.shape[1]
    return dict(
        nh=nh, ql=ql, kl=kl, cc=cc, rope=rope, nope=nope, vh=vh, vw=nh * vh, taps=taps,
        g_mix=row(norm_mix[i]), w_in=w_in_cat, g_ql=row(q_lora_norm[i]), w_uq=w_uq_cat, g_qvec=g_qvec, s_q=s_q,
        g_kv=row(kv_norm[i]), w_uk_pad=w_uk_pad.astype(BF16), w_uvt=w_uv[i].T.astype(BF16), g_kvec=g_kvec, s_k=s_k, g_kpe=g_kpe, s_kpe=s_kpe,
        w_ukt_pad=w_ukt_pad, w_ukt_perm=w_ukt_perm, p_mat=p_mat, w_bd=w_bd,
        conv_w=conv_w[i].astype(F32), conv_b=row(conv_b[i]), ln_g=row(conv_ln_g[i]), ln_b=row(conv_ln_b[i]),
        g_oc=row(norm_out_conv[i]), g_oa=row(norm_out_attn[i]),
        w_oa=w_out[i][:mw].astype(BF16), w_oc=w_out[i][mw:].astype(BF16),
        g_ffn=row(norm_ffn[i]), w_gate=w_gate[i].astype(BF16), w_up=w_up[i].astype(BF16),
        w_down=w_down[i].astype(BF16), g_ple=row(norm_ple[i]), w_pg=w_ple_gate[i].astype(BF16),
        w_pp=w_ple_proj[i].astype(BF16),
    )


def kernel(x_prompt, x_sample, cache_ckv, cache_kpe, state_conv, page_table, p_prompt, p_sample, norm_mix, w_in, q_lora_norm, w_uq, kv_norm, w_uk, w_uv, q_nope_norm, k_nope_norm, q_pe_norm, k_pe_norm, conv_w, conv_b, conv_ln_g, conv_ln_b, norm_out_attn, norm_out_conv, w_out, norm_ffn, w_gate, w_up, w_down, norm_ple, w_ple_gate, w_ple_proj):
    B, T, D = x_prompt.shape
    Bd, Tq, _ = x_sample.shape
    depth = w_in.shape[0]
    past_len = page_table.shape[1] * cache_ckv.shape[2]
    rope = q_pe_norm.shape[1]
    taps = conv_w.shape[1]
    tabq_p, tabk_p = _rope_tables(jnp.arange(T), rope)
    tabq_s, tabk_s = _rope_tables(past_len + jnp.arange(Tq), rope)
    tabq_s, tabk_s = jnp.tile(tabq_s, (Bd, 1)), jnp.tile(tabk_s, (Bd, 1))

    xp, xs = x_prompt, x_sample.reshape(1, Bd * Tq, D)
    outs = [[] for _ in range(6)]
    for i in range(depth):
        p = _layer_params(i, norm_mix, w_in, q_lora_norm, w_uq, kv_norm, w_uk, w_uv, q_nope_norm, k_nope_norm,
                          q_pe_norm, k_pe_norm, conv_w, conv_b, conv_ln_g, conv_ln_b, norm_out_attn,
                          norm_out_conv, w_out, norm_ffn, w_gate, w_up, w_down, norm_ple, w_ple_gate, w_ple_proj)
        nh, kl, cc = p["nh"], p["kl"], p["cc"]
        q, k, vt, ckv, kpe, glu = _in_proj(xp, tabq_p, tabk_p, p, BF16)
        attn = _flash(q, k, vt, p["vh"])
        xp = _post(xp, attn, p_prompt[i], glu, p, sample=False)
        outs[0].append(ckv)
        outs[1].append(kpe)
        outs[2].append(glu[:, T - (taps - 1):, :])
        q_s, _, _, ckv_s, kpe_s, glu_s = _in_proj(xs, tabq_s, tabk_s, p, F32)
        a_s, qpe_s = _absorb(q_s[0], p)
        ckv_s = ckv_s.reshape(Bd, Tq, kl)
        kpe_s = kpe_s.reshape(Bd, Tq, rope)
        page = cache_kpe.shape[2]
        knewt = jnp.pad(jnp.swapaxes(kpe_s, 1, 2), ((0, 0), (0, 0), (0, page - Tq)))
        o_lat = _paged_attn(page_table, a_s.reshape(Bd, Tq * nh, kl), qpe_s.reshape(Bd, Tq * nh, rope),
                            ckv_s, knewt, cache_ckv, jnp.swapaxes(cache_kpe, 2, 3), p, i)
        attn_s = _sproj(o_lat.reshape(Bd * Tq, nh * kl), p["w_bd"])
        xc = jnp.concatenate([state_conv[i], glu_s.reshape(Bd, Tq, cc)], axis=1)
        xs = _post(xs, attn_s[None], p_sample[i].reshape(1, Bd * Tq, -1), xc, p, sample=True)
        outs[3].append(ckv_s)
        outs[4].append(kpe_s)
        outs[5].append(xc[:, Tq:, :])
    return (xp, xs.reshape(Bd, Tq, D), jnp.stack(outs[0]), jnp.stack(outs[1]), jnp.stack(outs[2]),
            jnp.stack(outs[3]), jnp.stack(outs[4]), jnp.stack(outs[5]))
```

```python
import functools
import math

import jax
import jax.numpy as jnp
import numpy as np
from jax import lax
from jax.experimental import pallas as pl
from jax.experimental.pallas import tpu as pltpu

F32 = jnp.float32
BF16 = jnp.bfloat16
EPS = 1e-6
ROPE_THETA = 10000.0
LOG2E = math.log2(math.e)
NEG_INF = -1e30

LANES = 128
SUBLANES = 8
MXU_DIM = 256
BF16_ROWS = 16
VMEM_LIMIT = 56 * 1024 * 1024
TOKEN_TILE = 512
FLASH_Q_BLOCKS = 2
CONV_ROWS = 64
CONV_HALO = 32
PAGES_PER_CHUNK = 4
PAGED_GROUP = 4


def _rms(x, g):
    return x * lax.rsqrt(jnp.mean(x * x, axis=-1, keepdims=True) + EPS) * g


def _dot(a, b):
    return jnp.dot(a, b, preferred_element_type=F32)


def _dot_t(a, b):
    return lax.dot_general(a, b, (((1,), (1,)), ((), ())), preferred_element_type=F32)


def _seg_ms(x, s_ref):
    x2 = (x * x).astype(BF16)
    w = s_ref.shape[0]
    parts = [_dot(x2[:, j * w:(j + 1) * w], s_ref[...]) for j in range(x.shape[1] // w)]
    return parts[0] if len(parts) == 1 else jnp.concatenate(parts, axis=1)


def _in_body(x_ref, tab_ref, gmix_ref, win_ref, gql_ref, wuq_ref, gqv_ref, sq_ref,
             gkv_ref, wuk_ref, wuvt_ref, gkv_vec_ref, sk_ref, gkpe_ref, skpe_ref,
             q_ref, k_ref, vt_ref, ckv_ref, kpe_ref, glu_ref, *, ql, kl, cc, nh, rope):
    x = x_ref[0]
    h = _rms(x, gmix_ref[...]).astype(BF16)
    z = _dot(h, win_ref[...])
    o_a = ql + kl
    o_g = o_a + cc
    o_k = o_g + cc
    glu_ref[0] = z[:, o_a:o_g] * jax.nn.sigmoid(z[:, o_g:o_k])

    cqn = _rms(z[:, :ql], gql_ref[...]).astype(BF16)
    qa = _dot(cqn, wuq_ref[...])
    tab = tab_ref[...]
    lane = lax.broadcasted_iota(jnp.int32, tab.shape, 1)
    tq = jnp.concatenate([jnp.where(lane >= LANES - 2 * rope, tab, 1.0)] * nh, axis=1)
    q = qa * lax.rsqrt(_seg_ms(qa, sq_ref) + EPS) * gqv_ref[...] * tq
    q_ref[0] = q.astype(q_ref.dtype)

    ckvn = _rms(z[:, ql:o_a], gkv_ref[...])
    ckv_ref[0] = ckvn
    cb = ckvn.astype(BF16)
    vt_ref[0, 0] = _dot_t(wuvt_ref[...], cb).astype(vt_ref.dtype)
    ka = _dot(cb, wuk_ref[...])
    kn = ka * lax.rsqrt(_seg_ms(ka, sk_ref) + EPS) * gkv_vec_ref[...]

    kp = z[:, o_k:o_k + LANES]
    t = kp * lax.rsqrt(_seg_ms(kp, skpe_ref) + EPS) * gkpe_ref[...] * tab
    kd = t + pltpu.roll(t, rope, axis=1)
    kpe_ref[0] = kd[:, :rope]
    kdm = jnp.where(lane >= LANES - 2 * rope, kd, 0.0)
    k_ref[0] = (kn + jnp.concatenate([kdm] * nh, axis=1)).astype(k_ref.dtype)


def _in_proj(x, tab, p, q_dtype):
    B, T, D = x.shape
    tm = min(TOKEN_TILE, T)
    nh, ql, kl, cc, rope, vw = p["nh"], p["ql"], p["kl"], p["cc"], p["rope"], p["vw"]
    const = lambda a: pl.BlockSpec(a.shape, lambda b, i: (0,) * a.ndim, pipeline_mode=pl.Buffered(1))
    tok = lambda w: pl.BlockSpec((1, tm, w), lambda b, i: (b, i, 0))
    tab_spec = pl.BlockSpec((tm, LANES), lambda b, i: (i, 0))
    consts = [p["g_mix"], p["w_in"], p["g_ql"], p["w_uq"], p["g_qvec"], p["s_q"],
              p["g_kv"], p["w_uk_pad"], p["w_uvt"], p["g_kvec"], p["s_k"], p["g_kpe"], p["s_kpe"]]
    out_shape = (
        jax.ShapeDtypeStruct((B, T, nh * LANES), q_dtype),
        jax.ShapeDtypeStruct((B, T, nh * LANES), BF16),
        jax.ShapeDtypeStruct((B, T // tm, vw, tm), BF16),
        jax.ShapeDtypeStruct((B, T, kl), F32),
        jax.ShapeDtypeStruct((B, T, rope), F32),
        jax.ShapeDtypeStruct((B, T, cc), F32),
    )
    return pl.pallas_call(
        functools.partial(_in_body, ql=ql, kl=kl, cc=cc, nh=nh, rope=rope),
        grid=(B, T // tm),
        in_specs=[tok(D), tab_spec] + [const(a) for a in consts],
        out_specs=[tok(nh * LANES), tok(nh * LANES), pl.BlockSpec((1, 1, vw, tm), lambda b, i: (b, i, 0, 0)), tok(kl), tok(rope), tok(cc)],
        out_shape=out_shape,
        compiler_params=pltpu.CompilerParams(
            dimension_semantics=("parallel", "parallel"), vmem_limit_bytes=VMEM_LIMIT),
        name="in_proj",
    )(x, tab, *consts)


def _flash_body(q_ref, k_ref, vt_ref, o_ref, s_ref, mc_ref, m_ref, acc_ref, *, tk, tq, vh):
    i = pl.program_id(2)
    ns = tq // MXU_DIM
    nkb = tq // tk
    m_ref[...] = jnp.full(m_ref.shape, NEG_INF, F32)
    acc_ref[...] = jnp.zeros(acc_ref.shape, F32)
    ones = jnp.ones((acc_ref.shape[1] - vh, tk), BF16)

    def scores(h, j, strips):
        k = k_ref[0, pl.ds(pl.multiple_of(j * tk, tk), tk), h * LANES:(h + 1) * LANES]
        for s in strips:
            sl = slice(s * MXU_DIM, (s + 1) * MXU_DIM)
            st = _dot_t(k, q_ref[0, sl, h * LANES:(h + 1) * LANES])
            s_ref[h, :, sl] = st
            mc_ref[h, :, sl] = jnp.max(st, axis=0, keepdims=True)

    def softmax_pv(h, j, s, key_off):
        sl = slice(s * MXU_DIM, (s + 1) * MXU_DIM)
        st = s_ref[h, :, sl]
        if key_off is not None:
            key = lax.broadcasted_iota(jnp.int32, st.shape, 0) + key_off
            qry = lax.broadcasted_iota(jnp.int32, st.shape, 1) + s * MXU_DIM
            st = jnp.where(key <= qry, st, NEG_INF)
            m_cur = jnp.max(st, axis=0, keepdims=True)
        else:
            m_cur = mc_ref[h, :, sl]
        m_prev = m_ref[h, :, sl]
        m_new = jnp.maximum(m_prev, m_cur)
        alpha = jnp.exp2(m_prev - m_new)
        pt = jnp.exp2((st - m_new).astype(BF16))
        vt1 = jnp.concatenate([vt_ref[0, j, h * vh:(h + 1) * vh, :], ones], axis=0)
        acc_ref[h, :, sl] = alpha * acc_ref[h, :, sl] + _dot(vt1, pt)
        m_ref[h, :, sl] = m_new

    every = range(ns)
    scores(0, 0, every)

    def block(j, carry):
        scores(1, j, every)
        for s in every:
            softmax_pv(0, j, s, None)
        scores(0, j + 1, every)
        for s in every:
            softmax_pv(1, j, s, None)
        return carry

    j0 = nkb * i
    lax.fori_loop(0, j0, block, 0)
    for kb in range(nkb):
        act = [s for s in every if (s + 1) * MXU_DIM > kb * tk]
        off = lambda s, kb=kb: kb * tk if s * MXU_DIM < (kb + 1) * tk else None
        scores(1, j0 + kb, act)
        for s in act:
            softmax_pv(0, j0 + kb, s, off(s))
        if kb + 1 < nkb:
            scores(0, j0 + kb + 1, [s for s in every if (s + 1) * MXU_DIM > (kb + 1) * tk])
        for s in act:
            softmax_pv(1, j0 + kb, s, off(s))
    o_t = jnp.concatenate([acc_ref[h, :vh, :] / acc_ref[h, vh:vh + 1, :] for h in range(2)], axis=0)
    o_ref[0] = o_t.T


def _flash(q, k, vt, vh):
    B, T, W = q.shape
    nhp = W // (2 * LANES)
    tk = vt.shape[3]
    tq = math.gcd(T, FLASH_Q_BLOCKS * tk)
    return pl.pallas_call(
        functools.partial(_flash_body, tk=tk, tq=tq, vh=vh),
        grid=(B, nhp, T // tq),
        in_specs=[
            pl.BlockSpec((1, tq, 2 * LANES), lambda b, hp, i: (b, i, hp)),
            pl.BlockSpec((1, T, 2 * LANES), lambda b, hp, i: (b, 0, hp)),
            pl.BlockSpec((1, T // tk, 2 * vh, tk), lambda b, hp, i: (b, 0, hp, 0)),
        ],
        out_specs=pl.BlockSpec((1, tq, 2 * vh), lambda b, hp, i: (b, i, hp)),
        out_shape=jax.ShapeDtypeStruct((B, T, nhp * 2 * vh), F32),
        scratch_shapes=[pltpu.VMEM((2, tk, tq), F32), pltpu.VMEM((2, 1, tq), F32),
                        pltpu.VMEM((2, 1, tq), F32), pltpu.VMEM((2, vh + BF16_ROWS, tq), F32)],
        compiler_params=pltpu.CompilerParams(
            dimension_semantics=("parallel", "parallel", "arbitrary"), vmem_limit_bytes=VMEM_LIMIT),
        name="flash",
    )(q, k, vt)


def _absorb_body(q_ref, gk_ref, wukt_ref, pmat_ref, a_ref, qpe_ref, *, nh, kl):
    q = q_ref[...]
    qpe_ref[...] = _dot(q.astype(BF16), pmat_ref[...]).astype(qpe_ref.dtype)
    qg = (q * gk_ref[...]).astype(BF16)
    for h in range(nh):
        a_ref[:, h * kl:(h + 1) * kl] = _dot(qg[:, h * LANES:(h + 1) * LANES], wukt_ref[h]).astype(a_ref.dtype)


def _absorb(q, p):
    N, W = q.shape
    nh, kl, rope = p["nh"], p["kl"], p["rope"]
    tm = min(256, N)
    const = lambda a: pl.BlockSpec(a.shape, lambda i: (0,) * a.ndim, pipeline_mode=pl.Buffered(1))
    consts = [p["g_kvec"], p["w_ukt_pad"], p["p_mat"]]
    return pl.pallas_call(
        functools.partial(_absorb_body, nh=nh, kl=kl),
        grid=(N // tm,),
        in_specs=[pl.BlockSpec((tm, W), lambda i: (i, 0))] + [const(a) for a in consts],
        out_specs=[pl.BlockSpec((tm, nh * kl), lambda i: (i, 0)),
                   pl.BlockSpec((tm, nh * rope), lambda i: (i, 0))],
        out_shape=(jax.ShapeDtypeStruct((N, nh * kl), BF16), jax.ShapeDtypeStruct((N, nh * rope), BF16)),
        compiler_params=pltpu.CompilerParams(dimension_semantics=("parallel",), vmem_limit_bytes=VMEM_LIMIT),
        name="absorb",
    )(q, *consts)


def _paged_body(pt_ref, a_ref, qpe_ref, wukt_ref, cnew_ref, knewt_ref, cache_c_ref, cache_kt_ref, o_ref,
                cbuf, kbuf, sem, s_ref, cb_ref, m_ref, l_ref, acc_ref, *, layer, n_pages, page, nh, nope):
    g = pl.program_id(0)
    ng = pl.num_programs(0)
    grp, rows, _ = a_ref.shape
    ch = PAGES_PER_CHUNK
    n_chunks = n_pages // ch

    def copies(gg, cc, slot):
        out = []
        for e in range(grp):
            for i in range(ch):
                pg = pt_ref[(gg * grp + e) * n_pages + cc * ch + i]
                out.append(pltpu.make_async_copy(cache_c_ref.at[layer, pg],
                                                 cbuf.at[slot, e, pl.ds(i * page, page)], sem.at[slot, 0]))
                out.append(pltpu.make_async_copy(cache_kt_ref.at[layer, pg],
                                                 kbuf.at[slot, e, :, pl.ds(i * page, page)], sem.at[slot, 1]))
        return out

    def fetch(gg, cc, slot):
        for c in copies(gg, cc, slot):
            c.start()

    @pl.when(g == 0)
    def _():
        fetch(0, 0, 0)

    m_ref[...] = jnp.full(m_ref.shape, NEG_INF, F32)
    l_ref[...] = jnp.zeros(l_ref.shape, F32)
    acc_ref[...] = jnp.zeros(acc_ref.shape, F32)

    def scores(e, ckv, kpet):
        cb = ckv.astype(BF16)
        kt = _dot_t(wukt_ref[...], cb)
        ss = jnp.sum((kt * kt).reshape(nope, nh, kt.shape[1]), axis=0)
        r = lax.rsqrt(ss * (1.0 / nope) + EPS)
        s = (_dot_t(a_ref[e], cb) * jnp.concatenate([r] * (rows // nh), axis=0)
             + _dot(qpe_ref[e], kpet.astype(BF16)))
        return s, cb

    def softmax_pv(e, s, cb):
        m_prev = m_ref[e]
        m_new = jnp.maximum(m_prev, jnp.max(s, axis=-1, keepdims=True))
        alpha = jnp.exp2(m_prev - m_new)
        pr = jnp.exp2(s - m_new)
        l_ref[e] = alpha * l_ref[e] + jnp.sum(pr, axis=-1, keepdims=True)
        acc_ref[e] = alpha * acc_ref[e] + _dot(pr.astype(BF16), cb)
        m_ref[e] = m_new

    def stage_scores(e, slot):
        s, cb = scores(e, cbuf[slot, e], kbuf[slot, e])
        s_ref[e] = s
        cb_ref[e] = cb

    def stage_softmax(e):
        softmax_pv(e, s_ref[e], cb_ref[e])

    def chunk(c, first):
        slot = c % 2
        wrap = c + 1 == n_chunks
        fetch(jnp.where(wrap, jnp.minimum(g + 1, ng - 1), g), jnp.where(wrap, 0, c + 1), 1 - slot)
        for cp in copies(g, c, slot):
            cp.wait()
        for e in range(grp):
            stage_scores(e, slot)
            if e > 0 or not first:
                stage_softmax((e - 1) % grp)

    chunk(0, True)

    def later_chunk(c, carry):
        chunk(c, False)
        return carry

    lax.fori_loop(1, n_chunks, later_chunk, 0)
    stage_softmax(grp - 1)

    @pl.when(g == ng - 1)
    def _():
        for cp in copies(g, 0, 0):
            cp.wait()

    tq = cnew_ref.shape[1]
    zpad = jnp.zeros((page - tq, cnew_ref.shape[2]), F32)
    qi = lax.broadcasted_iota(jnp.int32, (rows, page), 0) // nh
    kj = lax.broadcasted_iota(jnp.int32, (rows, page), 1)
    for e in range(grp):
        s, cb = scores(e, jnp.concatenate([cnew_ref[e], zpad], axis=0), knewt_ref[e])
        softmax_pv(e, jnp.where(kj <= qi, s, NEG_INF), cb)
        o_ref[e] = acc_ref[e] / l_ref[e]


def _paged_attn(page_table, a, qpe, cnew, knewt, cache_c, cache_kt, p, layer):
    Bd, rows, kl = a.shape
    nh, rope, nope = p["nh"], p["rope"], p["nope"]
    n_pages = page_table.shape[1]
    page = cache_c.shape[2]
    grp = math.gcd(PAGED_GROUP, Bd)
    assert n_pages % (2 * PAGES_PER_CHUNK) == 0, "chunk slots alternate per grid step"
    keys = PAGES_PER_CHUNK * page
    tq = cnew.shape[1]
    row_blk = lambda r, w: pl.BlockSpec((grp, r, w), lambda b, pt: (b, 0, 0))
    grid_spec = pltpu.PrefetchScalarGridSpec(
        num_scalar_prefetch=1,
        grid=(Bd // grp,),
        in_specs=[
            row_blk(rows, kl),
            row_blk(rows, rope),
            pl.BlockSpec(p["w_ukt_perm"].shape, lambda b, pt: (0, 0), pipeline_mode=pl.Buffered(1)),
            row_blk(tq, kl),
            row_blk(rope, page),
            pl.BlockSpec(memory_space=pl.ANY),
            pl.BlockSpec(memory_space=pl.ANY),
        ],
        out_specs=row_blk(rows, kl),
        scratch_shapes=[
            pltpu.VMEM((2, grp, keys, kl), F32),
            pltpu.VMEM((2, grp, rope, keys), F32),
            pltpu.SemaphoreType.DMA((2, 2)),
            pltpu.VMEM((grp, rows, keys), F32),
            pltpu.VMEM((grp, keys, kl), BF16),
            pltpu.VMEM((grp, rows, 1), F32),
            pltpu.VMEM((grp, rows, 1), F32),
            pltpu.VMEM((grp, rows, kl), F32),
        ],
    )
    return pl.pallas_call(
        functools.partial(_paged_body, layer=layer, n_pages=n_pages, page=page, nh=nh, nope=nope),
        grid_spec=grid_spec,
        out_shape=jax.ShapeDtypeStruct((Bd, rows, kl), F32),
        compiler_params=pltpu.CompilerParams(dimension_semantics=("arbitrary",), vmem_limit_bytes=VMEM_LIMIT),
        name="paged_attn",
    )(page_table.reshape(-1), a, qpe, p["w_ukt_perm"], cnew, knewt, cache_c, cache_kt)


def _sproj_body(o_ref, w_ref, out_ref):
    out_ref[...] = _dot(o_ref[...].astype(BF16), w_ref[...])


def _sproj(o_flat, w_bd):
    N, K = o_flat.shape
    tm = min(TOKEN_TILE, N)
    return pl.pallas_call(
        _sproj_body,
        grid=(N // tm,),
        in_specs=[pl.BlockSpec((tm, K), lambda i: (i, 0)),
                  pl.BlockSpec(w_bd.shape, lambda i: (0, 0), pipeline_mode=pl.Buffered(1))],
        out_specs=pl.BlockSpec((tm, w_bd.shape[1]), lambda i: (i, 0)),
        out_shape=jax.ShapeDtypeStruct((N, w_bd.shape[1]), F32),
        compiler_params=pltpu.CompilerParams(dimension_semantics=("parallel",), vmem_limit_bytes=VMEM_LIMIT),
        name="sproj",
    )(o_flat, w_bd)


def _post_tail(y, x, attn, pe, refs, out_ref):
    (lng_ref, lnb_ref, goc_ref, goa_ref, woa_ref, woc_ref, gffn_ref, wg_ref, wu_ref, wd_ref,
     gple_ref, wpg_ref, wpp_ref) = refs
    mu = jnp.mean(y, axis=-1, keepdims=True)
    yc = y - mu
    yn = yc * lax.rsqrt(jnp.mean(yc * yc, axis=-1, keepdims=True) + EPS) * lng_ref[...] + lnb_ref[...]
    conv = yn * jax.nn.sigmoid(yn)
    x = x + _dot(_rms(attn, goa_ref[...]).astype(BF16), woa_ref[...])
    x = x + _dot(_rms(conv, goc_ref[...]).astype(BF16), woc_ref[...])
    h = _rms(x, gffn_ref[...]).astype(BF16)
    g = _dot(h, wg_ref[...])
    u = _dot(h, wu_ref[...])
    x = x + _dot((g * jax.nn.sigmoid(g) * u).astype(BF16), wd_ref[...])
    gate = jax.nn.sigmoid(_dot(_rms(x, gple_ref[...]).astype(BF16), wpg_ref[...]))
    out_ref[0] = x + gate * _dot(pe.astype(BF16), wpp_ref[...])


def _post_prompt_body(x_ref, attn_ref, pe_ref, glu_ref, prev_ref, cw_ref, cb_ref, *rest, taps):
    refs, out_ref, xs_ref, sh_ref, y_ref = rest[:-4], rest[-4], rest[-3], rest[-2], rest[-1]
    i = pl.program_id(1)
    tm = glu_ref.shape[1]
    xs_ref[0:CONV_HALO, :] = jnp.where(i > 0, prev_ref[0], 0.0)
    xs_ref[CONV_HALO:, :] = glu_ref[0]
    off = CONV_HALO - (taps - 1)
    span = sh_ref.shape[1] - SUBLANES * ((taps - 1) // SUBLANES)
    for h0 in range(0, tm, span):
        for b in range(min(SUBLANES, taps)):
            nrow = span + SUBLANES * ((taps - 1 - b) // SUBLANES)
            sh_ref[b, 0:nrow, :] = xs_ref[h0 + off + b:h0 + off + b + nrow, :]
        for r0 in range(0, span, CONV_ROWS):
            acc = jnp.broadcast_to(cb_ref[...], (CONV_ROWS, cb_ref.shape[1]))
            for k in range(taps):
                a, b = divmod(k, SUBLANES)
                acc = acc + cw_ref[k:k + 1, :] * sh_ref[b, r0 + SUBLANES * a:r0 + SUBLANES * a + CONV_ROWS, :]
            y_ref[h0 + r0:h0 + r0 + CONV_ROWS, :] = acc
    _post_tail(y_ref[...], x_ref[0], attn_ref[0], pe_ref[0], refs, out_ref)


def _post_sample_body(x_ref, attn_ref, pe_ref, xc_ref, cw_ref, cb_ref, *rest, taps):
    refs, out_ref = rest[:-1], rest[-1]
    nb, _, cc = xc_ref.shape
    tq = x_ref.shape[1] // nb
    acc = jnp.broadcast_to(cb_ref[...].reshape(1, 1, cc), (nb, tq, cc))
    for k in range(taps):
        acc = acc + cw_ref[k:k + 1, :].reshape(1, 1, cc) * xc_ref[:, k:k + tq, :]
    _post_tail(acc.reshape(nb * tq, cc), x_ref[0], attn_ref[0], pe_ref[0], refs, out_ref)


def _post(x, attn, pe, conv_in, p, sample):
    B, T, D = x.shape
    tm = min(TOKEN_TILE, T)
    cc, taps = p["cc"], p["taps"]
    const = lambda a: pl.BlockSpec(a.shape, lambda b, i: (0,) * a.ndim, pipeline_mode=pl.Buffered(1))
    tok = lambda w: pl.BlockSpec((1, tm, w), lambda b, i: (b, i, 0))
    consts = [p["conv_w"], p["conv_b"], p["ln_g"], p["ln_b"], p["g_oc"], p["g_oa"], p["w_oa"], p["w_oc"],
              p["g_ffn"], p["w_gate"], p["w_up"], p["w_down"], p["g_ple"], p["w_pg"], p["w_pp"]]
    if sample:
        tq = conv_in.shape[1] - (taps - 1)
        nb = tm // tq
        conv_specs = [pl.BlockSpec((nb,) + conv_in.shape[1:], lambda b, i: (i, 0, 0))]
        conv_args = [conv_in]
        body = functools.partial(_post_sample_body, taps=taps)
        scratch = []
    else:
        per = tm // CONV_HALO
        conv_specs = [tok(cc),
                      pl.BlockSpec((1, CONV_HALO, cc), lambda b, i: (b, jnp.maximum(i * per - 1, 0), 0))]
        conv_args = [conv_in, conv_in]
        body = functools.partial(_post_prompt_body, taps=taps)
        span = max(CONV_ROWS, tm // 2)
        scratch = [pltpu.VMEM((tm + CONV_HALO, cc), F32),
                   pltpu.VMEM((SUBLANES, span + SUBLANES * ((taps - 1) // SUBLANES), cc), F32),
                   pltpu.VMEM((tm, cc), F32)]
    return pl.pallas_call(
        body,
        grid=(B, T // tm),
        in_specs=[tok(D), tok(attn.shape[2]), tok(pe.shape[2])] + conv_specs + [const(a) for a in consts],
        out_specs=tok(D),
        out_shape=jax.ShapeDtypeStruct((B, T, D), F32),
        scratch_shapes=scratch,
        compiler_params=pltpu.CompilerParams(
            dimension_semantics=("parallel", "parallel"), vmem_limit_bytes=VMEM_LIMIT),
        name="post_sample" if sample else "post_prompt",
    )(x, attn, pe, *conv_args, *consts)


def _rope_table(pos, rope):
    inv = 1.0 / (ROPE_THETA ** (np.arange(0, rope, 2, dtype=np.float64) / rope))
    ang = np.asarray(pos, np.float64)[:, None] * inv[None, :]
    cos, sin = np.cos(ang), np.sin(ang)
    cos2 = np.concatenate([cos, cos], axis=1)
    sin2 = np.concatenate([-sin, sin], axis=1)
    return np.concatenate([cos2, sin2, cos2, sin2], axis=1).astype(np.float32)


def _swap_halves(w, axis):
    a, b = jnp.split(w, 2, axis=axis)
    return jnp.concatenate([b, a], axis=axis)


def _layer_params(i, norm_mix, w_in, q_lora_norm, w_uq, kv_norm, w_uk, w_uv, q_nope_norm, k_nope_norm,
                  q_pe_norm, k_pe_norm, conv_w, conv_b, conv_ln_g, conv_ln_b, norm_out_attn, norm_out_conv,
                  w_out, norm_ffn, w_gate, w_up, w_down, norm_ple, w_ple_gate, w_ple_proj):
    ql, kl = q_lora_norm.shape[1], kv_norm.shape[1]
    nope, rope = q_nope_norm.shape[1], q_pe_norm.shape[1]
    cc, taps = conv_w.shape[2], conv_w.shape[1]
    nh = w_uk.shape[2] // nope
    vh = w_uv.shape[2] // nh
    assert nope + 2 * rope == LANES and 4 * rope == LANES and 2 * vh == LANES and nh % 2 == 0
    assert taps - 1 <= CONV_HALO
    scale = LOG2E / math.sqrt(nope + rope)
    row = lambda v: v.reshape(1, -1).astype(F32)

    wi = w_in[i]
    w_q, w_kv, w_kpe, w_a, w_g = jnp.split(wi, [ql, ql + kl, ql + kl + rope, ql + kl + rope + cc], axis=1)
    w_kpe_sw = _swap_halves(w_kpe, 1)
    w_in_cat = jnp.concatenate([w_q, w_kv, w_a, w_g, w_kpe, w_kpe_sw, w_kpe, w_kpe_sw], axis=1).astype(BF16)

    wq3 = w_uq[i].reshape(ql, nh, nope + rope)
    wq_pe = wq3[:, :, nope:]
    w_uq_cat = jnp.concatenate([wq3[:, :, :nope], wq_pe, _swap_halves(wq_pe, 2)], axis=2)
    w_uq_cat = w_uq_cat.reshape(ql, nh * LANES).astype(BF16)
    gq = jnp.concatenate([q_nope_norm[i], q_pe_norm[i], _swap_halves(q_pe_norm[i], 0)]) * scale
    g_qvec = row(jnp.tile(gq, nh))

    wk3 = w_uk[i].reshape(kl, nh, nope)
    w_uk_pad = jnp.concatenate([wk3, jnp.zeros((kl, nh, LANES - nope), F32)], axis=2).reshape(kl, nh * LANES)
    gk = jnp.concatenate([k_nope_norm[i], jnp.zeros((LANES - nope,), F32)])
    g_kvec = row(jnp.tile(gk, nh))
    g_kpe = row(jnp.tile(jnp.concatenate([k_pe_norm[i], _swap_halves(k_pe_norm[i], 0)]), 2))

    li = jnp.arange(LANES)
    blk_q = (jnp.where((li[:, None] < nope) & (li[None, :] < nope), 1.0 / nope, 0.0)
             + jnp.where((li[:, None] >= nope) & (li[:, None] < nope + rope) & (li[None, :] >= nope),
                         1.0 / rope, 0.0))
    blk_k = jnp.where((li[:, None] < nope) & (li[None, :] < nope), 1.0 / nope, 0.0)
    eye2 = jnp.eye(MXU_DIM // LANES, dtype=F32)
    s_q = jnp.kron(eye2, blk_q).astype(BF16)
    s_k = jnp.kron(eye2, blk_k).astype(BF16)
    s_kpe = jnp.where(li[:, None] < rope, 1.0 / rope, 0.0) * jnp.ones((1, LANES), F32)
    s_kpe = s_kpe.astype(BF16)

    w_ukt_pad = jnp.concatenate([jnp.transpose(wk3, (1, 2, 0)),
                                 jnp.zeros((nh, LANES - nope, kl), F32)], axis=1).astype(BF16)
    w_ukt_perm = jnp.transpose(wk3, (2, 1, 0)).reshape(nope * nh, kl).astype(BF16)
    src = jnp.arange(nh * LANES)
    lane_in = src % LANES
    dst = (src // LANES) * rope + (lane_in - nope) % rope
    p_mat = jnp.where((lane_in >= nope)[:, None] & (dst[:, None] == jnp.arange(nh * rope)[None, :]), 1.0, 0.0)
    p_mat = p_mat.astype(BF16)
    wv3 = w_uv[i].reshape(kl, nh, vh)
    w_bd = jnp.einsum("chv,hg->hcgv", wv3, jnp.eye(nh, dtype=F32)).reshape(nh * kl, nh * vh).astype(BF16)

    mw = norm_out_attn.shape[1]
    return dict(
        nh=nh, ql=ql, kl=kl, cc=cc, rope=rope, nope=nope, vh=vh, vw=nh * vh, taps=taps,
        g_mix=row(norm_mix[i]), w_in=w_in_cat, g_ql=row(q_lora_norm[i]), w_uq=w_uq_cat, g_qvec=g_qvec, s_q=s_q,
        g_kv=row(kv_norm[i]), w_uk_pad=w_uk_pad.astype(BF16), w_uvt=w_uv[i].T.astype(BF16), g_kvec=g_kvec, s_k=s_k, g_kpe=g_kpe, s_kpe=s_kpe,
        w_ukt_pad=w_ukt_pad, w_ukt_perm=w_ukt_perm, p_mat=p_mat, w_bd=w_bd,
        conv_w=conv_w[i].astype(F32), conv_b=row(conv_b[i]), ln_g=row(conv_ln_g[i]), ln_b=row(conv_ln_b[i]),
        g_oc=row(norm_out_conv[i]), g_oa=row(norm_out_attn[i]),
        w_oa=w_out[i][:mw].astype(BF16), w_oc=w_out[i][mw:].astype(BF16),
        g_ffn=row(norm_ffn[i]), w_gate=w_gate[i].astype(BF16), w_up=w_up[i].astype(BF16),
        w_down=w_down[i].astype(BF16), g_ple=row(norm_ple[i]), w_pg=w_ple_gate[i].astype(BF16),
        w_pp=w_ple_proj[i].astype(BF16),
    )


def kernel(x_prompt, x_sample, cache_ckv, cache_kpe, state_conv, page_table, p_prompt, p_sample, norm_mix, w_in, q_lora_norm, w_uq, kv_norm, w_uk, w_uv, q_nope_norm, k_nope_norm, q_pe_norm, k_pe_norm, conv_w, conv_b, conv_ln_g, conv_ln_b, norm_out_attn, norm_out_conv, w_out, norm_ffn, w_gate, w_up, w_down, norm_ple, w_ple_gate, w_ple_proj):
    B, T, D = x_prompt.shape
    Bd, Tq, _ = x_sample.shape
    depth = w_in.shape[0]
    past_len = page_table.shape[1] * cache_ckv.shape[2]
    rope = q_pe_norm.shape[1]
    taps = conv_w.shape[1]
    tab_p = _rope_table(np.arange(T), rope)
    tab_s = np.tile(_rope_table(past_len + np.arange(Tq), rope), (Bd, 1))

    xp, xs = x_prompt, x_sample.reshape(1, Bd * Tq, D)
    outs = [[] for _ in range(6)]
    for i in range(depth):
        p = _layer_params(i, norm_mix, w_in, q_lora_norm, w_uq, kv_norm, w_uk, w_uv, q_nope_norm, k_nope_norm,
                          q_pe_norm, k_pe_norm, conv_w, conv_b, conv_ln_g, conv_ln_b, norm_out_attn,
                          norm_out_conv, w_out, norm_ffn, w_gate, w_up, w_down, norm_ple, w_ple_gate, w_ple_proj)
        nh, kl, cc = p["nh"], p["kl"], p["cc"]
        q, k, vt, ckv, kpe, glu = _in_proj(xp, tab_p, p, BF16)
        attn = _flash(q, k, vt, p["vh"])
        xp = _post(xp, attn, p_prompt[i], glu, p, sample=False)
        outs[0].append(ckv)
        outs[1].append(kpe)
        outs[2].append(glu[:, T - (taps - 1):, :])
        q_s, _, _, ckv_s, kpe_s, glu_s = _in_proj(xs, tab_s, p, F32)
        a_s, qpe_s = _absorb(q_s[0], p)
        ckv_s = ckv_s.reshape(Bd, Tq, kl)
        kpe_s = kpe_s.reshape(Bd, Tq, rope)
        page = cache_kpe.shape[2]
        knewt = jnp.pad(jnp.swapaxes(kpe_s, 1, 2), ((0, 0), (0, 0), (0, page - Tq)))
        o_lat = _paged_attn(page_table, a_s.reshape(Bd, Tq * nh, kl), qpe_s.reshape(Bd, Tq * nh, rope),
                            ckv_s, knewt, cache_ckv, jnp.swapaxes(cache_kpe, 2, 3), p, i)
        attn_s = _sproj(o_lat.reshape(Bd * Tq, nh * kl), p["w_bd"])
        xc = jnp.concatenate([state_conv[i], glu_s.reshape(Bd, Tq, cc)], axis=1)
        xs = _post(xs, attn_s[None], p_sample[i].reshape(1, Bd * Tq, -1), xc, p, sample=True)
        outs[3].append(ckv_s)
        outs[4].append(kpe_s)
        outs[5].append(xc[:, Tq:, :])
    return (xp, xs.reshape(Bd, Tq, D), jnp.stack(outs[0]), jnp.stack(outs[1]), jnp.stack(outs[2]),
            jnp.stack(outs[3]), jnp.stack(outs[4]), jnp.stack(outs[5]))
```

```python
import functools
import math

import jax
import jax.numpy as jnp
import numpy as np
from jax import lax
from jax.experimental import pallas as pl
from jax.experimental.pallas import tpu as pltpu

F32 = jnp.float32
BF16 = jnp.bfloat16
EPS = 1e-6
ROPE_THETA = 10000.0
LOG2E = math.log2(math.e)
NEG_INF = -1e30

LANES = 128
SUBLANES = 8
MXU_DIM = 256
BF16_ROWS = 16
VMEM_LIMIT = 56 * 1024 * 1024
TOKEN_TILE = 512
FLASH_Q_BLOCKS = 4
CONV_ROWS = 64
CONV_HALO = 32
PAGES_PER_CHUNK = 4
PAGED_GROUP = 4
PAGED_SLOTS = 3


def _rms(x, g):
    return x * lax.rsqrt(jnp.mean(x * x, axis=-1, keepdims=True) + EPS) * g


def _dot(a, b):
    return jnp.dot(a, b, preferred_element_type=F32)


def _dot_t(a, b):
    return lax.dot_general(a, b, (((1,), (1,)), ((), ())), preferred_element_type=F32)


def _seg_ms(x, s_ref):
    x2 = (x * x).astype(BF16)
    w = s_ref.shape[0]
    parts = [_dot(x2[:, j * w:(j + 1) * w], s_ref[...]) for j in range(x.shape[1] // w)]
    return parts[0] if len(parts) == 1 else jnp.concatenate(parts, axis=1)


def _in_body(x_ref, tab_ref, gmix_ref, win_ref, gql_ref, wuq_ref, gqv_ref, sq_ref,
             gkv_ref, wuk_ref, wuvt_ref, gkv_vec_ref, sk_ref, gkpe_ref, skpe_ref,
             q_ref, k_ref, vt_ref, ckv_ref, kpe_ref, glu_ref, *, ql, kl, cc, nh, rope):
    x = x_ref[0]
    h = _rms(x, gmix_ref[...]).astype(BF16)
    z = _dot(h, win_ref[...])
    o_a = ql + kl
    o_g = o_a + cc
    o_k = o_g + cc
    glu_ref[0] = z[:, o_a:o_g] * jax.nn.sigmoid(z[:, o_g:o_k])

    cqn = _rms(z[:, :ql], gql_ref[...]).astype(BF16)
    qa = _dot(cqn, wuq_ref[...])
    tab = tab_ref[...]
    lane = lax.broadcasted_iota(jnp.int32, tab.shape, 1)
    tq = jnp.concatenate([jnp.where(lane >= LANES - 2 * rope, tab, 1.0)] * nh, axis=1)
    q = qa * lax.rsqrt(_seg_ms(qa, sq_ref) + EPS) * gqv_ref[...] * tq
    q_ref[0] = q.astype(q_ref.dtype)

    ckvn = _rms(z[:, ql:o_a], gkv_ref[...])
    ckv_ref[0] = ckvn
    cb = ckvn.astype(BF16)
    vt_ref[0, 0] = _dot_t(wuvt_ref[...], cb).astype(vt_ref.dtype)
    ka = _dot(cb, wuk_ref[...])
    kn = ka * lax.rsqrt(_seg_ms(ka, sk_ref) + EPS) * gkv_vec_ref[...]

    kp = z[:, o_k:o_k + LANES]
    t = kp * lax.rsqrt(_seg_ms(kp, skpe_ref) + EPS) * gkpe_ref[...] * tab
    kd = t + pltpu.roll(t, rope, axis=1)
    kpe_ref[0] = kd[:, :rope]
    kdm = jnp.where(lane >= LANES - 2 * rope, kd, 0.0)
    k_ref[0] = (kn + jnp.concatenate([kdm] * nh, axis=1)).astype(k_ref.dtype)


def _in_proj(x, tab, p, q_dtype):
    B, T, D = x.shape
    tm = min(TOKEN_TILE, T)
    nh, ql, kl, cc, rope, vw = p["nh"], p["ql"], p["kl"], p["cc"], p["rope"], p["vw"]
    const = lambda a: pl.BlockSpec(a.shape, lambda b, i: (0,) * a.ndim, pipeline_mode=pl.Buffered(1))
    tok = lambda w: pl.BlockSpec((1, tm, w), lambda b, i: (b, i, 0))
    tab_spec = pl.BlockSpec((tm, LANES), lambda b, i: (i, 0))
    consts = [p["g_mix"], p["w_in"], p["g_ql"], p["w_uq"], p["g_qvec"], p["s_q"],
              p["g_kv"], p["w_uk_pad"], p["w_uvt"], p["g_kvec"], p["s_k"], p["g_kpe"], p["s_kpe"]]
    out_shape = (
        jax.ShapeDtypeStruct((B, T, nh * LANES), q_dtype),
        jax.ShapeDtypeStruct((B, T, nh * LANES), BF16),
        jax.ShapeDtypeStruct((B, T // tm, vw, tm), BF16),
        jax.ShapeDtypeStruct((B, T, kl), F32),
        jax.ShapeDtypeStruct((B, T, rope), F32),
        jax.ShapeDtypeStruct((B, T, cc), F32),
    )
    return pl.pallas_call(
        functools.partial(_in_body, ql=ql, kl=kl, cc=cc, nh=nh, rope=rope),
        grid=(B, T // tm),
        in_specs=[tok(D), tab_spec] + [const(a) for a in consts],
        out_specs=[tok(nh * LANES), tok(nh * LANES), pl.BlockSpec((1, 1, vw, tm), lambda b, i: (b, i, 0, 0)), tok(kl), tok(rope), tok(cc)],
        out_shape=out_shape,
        compiler_params=pltpu.CompilerParams(
            dimension_semantics=("parallel", "parallel"), vmem_limit_bytes=VMEM_LIMIT),
        name="in_proj",
    )(x, tab, *consts)


def _flash_body(q_ref, k_ref, vt_ref, o_ref, s_ref, mc_ref, m_ref, acc_ref, *, tk, tq, vh):
    i = pl.program_id(2)
    ns = tq // MXU_DIM
    nkb = tq // tk
    m_ref[...] = jnp.full(m_ref.shape, NEG_INF, F32)
    acc_ref[...] = jnp.zeros(acc_ref.shape, F32)
    ones = jnp.ones((acc_ref.shape[1] - vh, tk), BF16)

    def scores(h, j, strips):
        k = k_ref[0, pl.ds(pl.multiple_of(j * tk, tk), tk), h * LANES:(h + 1) * LANES]
        for s in strips:
            sl = slice(s * MXU_DIM, (s + 1) * MXU_DIM)
            st = _dot_t(k, q_ref[0, sl, h * LANES:(h + 1) * LANES])
            s_ref[h, :, sl] = st
            mc_ref[h, :, sl] = jnp.max(st, axis=0, keepdims=True)

    def softmax_pv(h, j, s, key_off):
        sl = slice(s * MXU_DIM, (s + 1) * MXU_DIM)
        st = s_ref[h, :, sl]
        if key_off is not None:
            key = lax.broadcasted_iota(jnp.int32, st.shape, 0) + key_off
            qry = lax.broadcasted_iota(jnp.int32, st.shape, 1) + s * MXU_DIM
            st = jnp.where(key <= qry, st, NEG_INF)
            m_cur = jnp.max(st, axis=0, keepdims=True)
        else:
            m_cur = mc_ref[h, :, sl]
        m_prev = m_ref[h, :, sl]
        m_new = jnp.maximum(m_prev, m_cur)
        alpha = jnp.exp2(m_prev - m_new)
        pt = jnp.exp2((st - m_new).astype(BF16))
        vt1 = jnp.concatenate([vt_ref[0, j, h * vh:(h + 1) * vh, :], ones], axis=0)
        acc_ref[h, :, sl] = alpha * acc_ref[h, :, sl] + _dot(vt1, pt)
        m_ref[h, :, sl] = m_new

    every = range(ns)
    scores(0, 0, every)

    def block(j, carry):
        scores(1, j, every)
        for s in every:
            softmax_pv(0, j, s, None)
        scores(0, j + 1, every)
        for s in every:
            softmax_pv(1, j, s, None)
        return carry

    j0 = nkb * i
    lax.fori_loop(0, j0, block, 0)
    for kb in range(nkb):
        act = [s for s in every if (s + 1) * MXU_DIM > kb * tk]
        off = lambda s, kb=kb: kb * tk if s * MXU_DIM < (kb + 1) * tk else None
        scores(1, j0 + kb, act)
        for s in act:
            softmax_pv(0, j0 + kb, s, off(s))
        if kb + 1 < nkb:
            scores(0, j0 + kb + 1, [s for s in every if (s + 1) * MXU_DIM > (kb + 1) * tk])
        for s in act:
            softmax_pv(1, j0 + kb, s, off(s))
    o_t = jnp.concatenate([acc_ref[h, :vh, :] / acc_ref[h, vh:vh + 1, :] for h in range(2)], axis=0)
    o_ref[0] = o_t.T


def _flash(q, k, vt, vh):
    B, T, W = q.shape
    nhp = W // (2 * LANES)
    tk = vt.shape[3]
    tq = math.gcd(T, FLASH_Q_BLOCKS * tk)
    return pl.pallas_call(
        functools.partial(_flash_body, tk=tk, tq=tq, vh=vh),
        grid=(B, nhp, T // tq),
        in_specs=[
            pl.BlockSpec((1, tq, 2 * LANES), lambda b, hp, i: (b, i, hp)),
            pl.BlockSpec((1, T, 2 * LANES), lambda b, hp, i: (b, 0, hp)),
            pl.BlockSpec((1, T // tk, 2 * vh, tk), lambda b, hp, i: (b, 0, hp, 0)),
        ],
        out_specs=pl.BlockSpec((1, tq, 2 * vh), lambda b, hp, i: (b, i, hp)),
        out_shape=jax.ShapeDtypeStruct((B, T, nhp * 2 * vh), F32),
        scratch_shapes=[pltpu.VMEM((2, tk, tq), F32), pltpu.VMEM((2, 1, tq), F32),
                        pltpu.VMEM((2, 1, tq), F32), pltpu.VMEM((2, vh + BF16_ROWS, tq), F32)],
        compiler_params=pltpu.CompilerParams(
            dimension_semantics=("parallel", "parallel", "arbitrary"), vmem_limit_bytes=VMEM_LIMIT),
        name="flash",
    )(q, k, vt)


def _absorb_body(q_ref, gk_ref, wukt_ref, pmat_ref, a_ref, qpe_ref, *, nh, kl):
    q = q_ref[...]
    qpe_ref[...] = _dot(q.astype(BF16), pmat_ref[...]).astype(qpe_ref.dtype)
    qg = (q * gk_ref[...]).astype(BF16)
    for h in range(nh):
        a_ref[:, h * kl:(h + 1) * kl] = _dot(qg[:, h * LANES:(h + 1) * LANES], wukt_ref[h]).astype(a_ref.dtype)


def _absorb(q, p):
    N, W = q.shape
    nh, kl, rope = p["nh"], p["kl"], p["rope"]
    tm = min(256, N)
    const = lambda a: pl.BlockSpec(a.shape, lambda i: (0,) * a.ndim, pipeline_mode=pl.Buffered(1))
    consts = [p["g_kvec"], p["w_ukt_pad"], p["p_mat"]]
    return pl.pallas_call(
        functools.partial(_absorb_body, nh=nh, kl=kl),
        grid=(N // tm,),
        in_specs=[pl.BlockSpec((tm, W), lambda i: (i, 0))] + [const(a) for a in consts],
        out_specs=[pl.BlockSpec((tm, nh * kl), lambda i: (i, 0)),
                   pl.BlockSpec((tm, nh * rope), lambda i: (i, 0))],
        out_shape=(jax.ShapeDtypeStruct((N, nh * kl), BF16), jax.ShapeDtypeStruct((N, nh * rope), BF16)),
        compiler_params=pltpu.CompilerParams(dimension_semantics=("parallel",), vmem_limit_bytes=VMEM_LIMIT),
        name="absorb",
    )(q, *consts)


def _paged_body(pt_ref, a_ref, qpe_ref, wukt_ref, cnew_ref, knewt_ref, cache_c_ref, cache_kt_ref, o_ref,
                cbuf, kbuf, sem, s_ref, cb_ref, m_ref, l_ref, acc_ref, *, layer, n_pages, page, nh, nope):
    g = pl.program_id(0)
    ng = pl.num_programs(0)
    grp, rows, _ = a_ref.shape
    ch = PAGES_PER_CHUNK
    n_chunks = n_pages // ch

    def copies(gg, cc, slot):
        out = []
        for e in range(grp):
            for i in range(ch):
                pg = pt_ref[(gg * grp + e) * n_pages + cc * ch + i]
                out.append(pltpu.make_async_copy(cache_c_ref.at[layer, pg],
                                                 cbuf.at[slot, e, pl.ds(i * page, page)], sem.at[slot, 0]))
                out.append(pltpu.make_async_copy(cache_kt_ref.at[layer, pg],
                                                 kbuf.at[slot, e, :, pl.ds(i * page, page)], sem.at[slot, 1]))
        return out

    def fetch(gg, cc, slot):
        for c in copies(gg, cc, slot):
            c.start()

    nslot = cbuf.shape[0]
    ahead = nslot - 1

    @pl.when(g == 0)
    def _():
        for cc in range(ahead):
            fetch(0, cc, cc)

    m_ref[...] = jnp.full(m_ref.shape, NEG_INF, F32)
    l_ref[...] = jnp.zeros(l_ref.shape, F32)
    acc_ref[...] = jnp.zeros(acc_ref.shape, F32)

    def scores(e, ckv, kpet):
        cb = ckv.astype(BF16)
        kt = _dot_t(wukt_ref[...], cb)
        ss = jnp.sum((kt * kt).reshape(nope, nh, kt.shape[1]), axis=0)
        r = lax.rsqrt(ss * (1.0 / nope) + EPS)
        s = (_dot_t(a_ref[e], cb) * jnp.concatenate([r] * (rows // nh), axis=0)
             + _dot(qpe_ref[e], kpet.astype(BF16)))
        return s, cb

    def softmax_pv(e, s, cb):
        m_prev = m_ref[e]
        m_new = jnp.maximum(m_prev, jnp.max(s, axis=-1, keepdims=True))
        alpha = jnp.exp2(m_prev - m_new)
        pr = jnp.exp2(s - m_new)
        l_ref[e] = alpha * l_ref[e] + jnp.sum(pr, axis=-1, keepdims=True)
        acc_ref[e] = alpha * acc_ref[e] + _dot(pr.astype(BF16), cb)
        m_ref[e] = m_new

    def stage_scores(e, slot):
        s, cb = scores(e, cbuf[slot, e], kbuf[slot, e])
        s_ref[e] = s
        cb_ref[e] = cb

    def stage_softmax(e):
        softmax_pv(e, s_ref[e], cb_ref[e])

    def chunk(c, first):
        seq = g * n_chunks + c
        slot = seq % nslot
        nxt = c + ahead
        wrap = nxt >= n_chunks
        fetch(jnp.where(wrap, jnp.minimum(g + 1, ng - 1), g), jnp.where(wrap, nxt - n_chunks, nxt),
              (seq + ahead) % nslot)
        for cp in copies(g, c, slot):
            cp.wait()
        for e in range(grp):
            stage_scores(e, slot)
            if e > 0 or not first:
                stage_softmax((e - 1) % grp)

    chunk(0, True)

    def later_chunk(c, carry):
        chunk(c, False)
        return carry

    lax.fori_loop(1, n_chunks, later_chunk, 0)
    stage_softmax(grp - 1)

    @pl.when(g == ng - 1)
    def _():
        for cc in range(ahead):
            for cp in copies(g, cc, (ng * n_chunks + cc) % nslot):
                cp.wait()

    tq = cnew_ref.shape[1]
    zpad = jnp.zeros((page - tq, cnew_ref.shape[2]), F32)
    qi = lax.broadcasted_iota(jnp.int32, (rows, page), 0) // nh
    kj = lax.broadcasted_iota(jnp.int32, (rows, page), 1)
    for e in range(grp):
        s, cb = scores(e, jnp.concatenate([cnew_ref[e], zpad], axis=0), knewt_ref[e])
        softmax_pv(e, jnp.where(kj <= qi, s, NEG_INF), cb)
        o_ref[e] = acc_ref[e] / l_ref[e]


def _paged_attn(page_table, a, qpe, cnew, knewt, cache_c, cache_kt, p, layer):
    Bd, rows, kl = a.shape
    nh, rope, nope = p["nh"], p["rope"], p["nope"]
    n_pages = page_table.shape[1]
    page = cache_c.shape[2]
    grp = math.gcd(PAGED_GROUP, Bd)
    assert n_pages % PAGES_PER_CHUNK == 0 and n_pages // PAGES_PER_CHUNK >= PAGED_SLOTS
    keys = PAGES_PER_CHUNK * page
    tq = cnew.shape[1]
    row_blk = lambda r, w: pl.BlockSpec((grp, r, w), lambda b, pt: (b, 0, 0))
    grid_spec = pltpu.PrefetchScalarGridSpec(
        num_scalar_prefetch=1,
        grid=(Bd // grp,),
        in_specs=[
            row_blk(rows, kl),
            row_blk(rows, rope),
            pl.BlockSpec(p["w_ukt_perm"].shape, lambda b, pt: (0, 0), pipeline_mode=pl.Buffered(1)),
            row_blk(tq, kl),
            row_blk(rope, page),
            pl.BlockSpec(memory_space=pl.ANY),
            pl.BlockSpec(memory_space=pl.ANY),
        ],
        out_specs=row_blk(rows, kl),
        scratch_shapes=[
            pltpu.VMEM((PAGED_SLOTS, grp, keys, kl), F32),
            pltpu.VMEM((PAGED_SLOTS, grp, rope, keys), F32),
            pltpu.SemaphoreType.DMA((PAGED_SLOTS, 2)),
            pltpu.VMEM((grp, rows, keys), F32),
            pltpu.VMEM((grp, keys, kl), BF16),
            pltpu.VMEM((grp, rows, 1), F32),
            pltpu.VMEM((grp, rows, 1), F32),
            pltpu.VMEM((grp, rows, kl), F32),
        ],
    )
    return pl.pallas_call(
        functools.partial(_paged_body, layer=layer, n_pages=n_pages, page=page, nh=nh, nope=nope),
        grid_spec=grid_spec,
        out_shape=jax.ShapeDtypeStruct((Bd, rows, kl), F32),
        compiler_params=pltpu.CompilerParams(dimension_semantics=("arbitrary",), vmem_limit_bytes=VMEM_LIMIT),
        name="paged_attn",
    )(page_table.reshape(-1), a, qpe, p["w_ukt_perm"], cnew, knewt, cache_c, cache_kt)


def _sproj_body(o_ref, w_ref, out_ref):
    out_ref[...] = _dot(o_ref[...].astype(BF16), w_ref[...])


def _sproj(o_flat, w_bd):
    N, K = o_flat.shape
    tm = min(TOKEN_TILE, N)
    return pl.pallas_call(
        _sproj_body,
        grid=(N // tm,),
        in_specs=[pl.BlockSpec((tm, K), lambda i: (i, 0)),
                  pl.BlockSpec(w_bd.shape, lambda i: (0, 0), pipeline_mode=pl.Buffered(1))],
        out_specs=pl.BlockSpec((tm, w_bd.shape[1]), lambda i: (i, 0)),
        out_shape=jax.ShapeDtypeStruct((N, w_bd.shape[1]), F32),
        compiler_params=pltpu.CompilerParams(dimension_semantics=("parallel",), vmem_limit_bytes=VMEM_LIMIT),
        name="sproj",
    )(o_flat, w_bd)


def _post_tail(y, x, attn, pe, refs, out_ref):
    (lng_ref, lnb_ref, goc_ref, goa_ref, woa_ref, woc_ref, gffn_ref, wg_ref, wu_ref, wd_ref,
     gple_ref, wpg_ref, wpp_ref) = refs
    mu = jnp.mean(y, axis=-1, keepdims=True)
    yc = y - mu
    yn = yc * lax.rsqrt(jnp.mean(yc * yc, axis=-1, keepdims=True) + EPS) * lng_ref[...] + lnb_ref[...]
    conv = yn * jax.nn.sigmoid(yn)
    x = x + _dot(_rms(attn, goa_ref[...]).astype(BF16), woa_ref[...])
    x = x + _dot(_rms(conv, goc_ref[...]).astype(BF16), woc_ref[...])
    h = _rms(x, gffn_ref[...]).astype(BF16)
    g = _dot(h, wg_ref[...])
    u = _dot(h, wu_ref[...])
    x = x + _dot((g * jax.nn.sigmoid(g) * u).astype(BF16), wd_ref[...])
    gate = jax.nn.sigmoid(_dot(_rms(x, gple_ref[...]).astype(BF16), wpg_ref[...]))
    out_ref[0] = x + gate * _dot(pe.astype(BF16), wpp_ref[...])


def _post_prompt_body(x_ref, attn_ref, pe_ref, glu_ref, prev_ref, cw_ref, cb_ref, *rest, taps):
    refs, out_ref, xs_ref, sh_ref, y_ref = rest[:-4], rest[-4], rest[-3], rest[-2], rest[-1]
    i = pl.program_id(1)
    tm = glu_ref.shape[1]
    xs_ref[0:CONV_HALO, :] = jnp.where(i > 0, prev_ref[0], 0.0)
    xs_ref[CONV_HALO:, :] = glu_ref[0]
    off = CONV_HALO - (taps - 1)
    span = sh_ref.shape[1] - SUBLANES * ((taps - 1) // SUBLANES)
    for h0 in range(0, tm, span):
        for b in range(min(SUBLANES, taps)):
            nrow = span + SUBLANES * ((taps - 1 - b) // SUBLANES)
            sh_ref[b, 0:nrow, :] = xs_ref[h0 + off + b:h0 + off + b + nrow, :]
        for r0 in range(0, span, CONV_ROWS):
            acc = jnp.broadcast_to(cb_ref[...], (CONV_ROWS, cb_ref.shape[1]))
            for k in range(taps):
                a, b = divmod(k, SUBLANES)
                acc = acc + cw_ref[k:k + 1, :] * sh_ref[b, r0 + SUBLANES * a:r0 + SUBLANES * a + CONV_ROWS, :]
            y_ref[h0 + r0:h0 + r0 + CONV_ROWS, :] = acc
    _post_tail(y_ref[...], x_ref[0], attn_ref[0], pe_ref[0], refs, out_ref)


def _post_sample_body(x_ref, attn_ref, pe_ref, xc_ref, cw_ref, cb_ref, *rest, taps):
    refs, out_ref = rest[:-1], rest[-1]
    nb, _, cc = xc_ref.shape
    tq = x_ref.shape[1] // nb
    acc = jnp.broadcast_to(cb_ref[...].reshape(1, 1, cc), (nb, tq, cc))
    for k in range(taps):
        acc = acc + cw_ref[k:k + 1, :].reshape(1, 1, cc) * xc_ref[:, k:k + tq, :]
    _post_tail(acc.reshape(nb * tq, cc), x_ref[0], attn_ref[0], pe_ref[0], refs, out_ref)


def _post(x, attn, pe, conv_in, p, sample):
    B, T, D = x.shape
    tm = min(TOKEN_TILE, T)
    cc, taps = p["cc"], p["taps"]
    const = lambda a: pl.BlockSpec(a.shape, lambda b, i: (0,) * a.ndim, pipeline_mode=pl.Buffered(1))
    tok = lambda w: pl.BlockSpec((1, tm, w), lambda b, i: (b, i, 0))
    consts = [p["conv_w"], p["conv_b"], p["ln_g"], p["ln_b"], p["g_oc"], p["g_oa"], p["w_oa"], p["w_oc"],
              p["g_ffn"], p["w_gate"], p["w_up"], p["w_down"], p["g_ple"], p["w_pg"], p["w_pp"]]
    if sample:
        tq = conv_in.shape[1] - (taps - 1)
        nb = tm // tq
        conv_specs = [pl.BlockSpec((nb,) + conv_in.shape[1:], lambda b, i: (i, 0, 0))]
        conv_args = [conv_in]
        body = functools.partial(_post_sample_body, taps=taps)
        scratch = []
    else:
        per = tm // CONV_HALO
        conv_specs = [tok(cc),
                      pl.BlockSpec((1, CONV_HALO, cc), lambda b, i: (b, jnp.maximum(i * per - 1, 0), 0))]
        conv_args = [conv_in, conv_in]
        body = functools.partial(_post_prompt_body, taps=taps)
        span = max(CONV_ROWS, tm // 2)
        scratch = [pltpu.VMEM((tm + CONV_HALO, cc), F32),
                   pltpu.VMEM((SUBLANES, span + SUBLANES * ((taps - 1) // SUBLANES), cc), F32),
                   pltpu.VMEM((tm, cc), F32)]
    return pl.pallas_call(
        body,
        grid=(B, T // tm),
        in_specs=[tok(D), tok(attn.shape[2]), tok(pe.shape[2])] + conv_specs + [const(a) for a in consts],
        out_specs=tok(D),
        out_shape=jax.ShapeDtypeStruct((B, T, D), F32),
        scratch_shapes=scratch,
        compiler_params=pltpu.CompilerParams(
            dimension_semantics=("parallel", "parallel"), vmem_limit_bytes=VMEM_LIMIT),
        name="post_sample" if sample else "post_prompt",
    )(x, attn, pe, *conv_args, *consts)


def _rope_table(pos, rope):
    inv = 1.0 / (ROPE_THETA ** (np.arange(0, rope, 2, dtype=np.float64) / rope))
    ang = np.asarray(pos, np.float64)[:, None] * inv[None, :]
    cos, sin = np.cos(ang), np.sin(ang)
    cos2 = np.concatenate([cos, cos], axis=1)
    sin2 = np.concatenate([-sin, sin], axis=1)
    return np.concatenate([cos2, sin2, cos2, sin2], axis=1).astype(np.float32)


def _swap_halves(w, axis):
    a, b = jnp.split(w, 2, axis=axis)
    return jnp.concatenate([b, a], axis=axis)


def _layer_params(i, norm_mix, w_in, q_lora_norm, w_uq, kv_norm, w_uk, w_uv, q_nope_norm, k_nope_norm,
                  q_pe_norm, k_pe_norm, conv_w, conv_b, conv_ln_g, conv_ln_b, norm_out_attn, norm_out_conv,
                  w_out, norm_ffn, w_gate, w_up, w_down, norm_ple, w_ple_gate, w_ple_proj):
    ql, kl = q_lora_norm.shape[1], kv_norm.shape[1]
    nope, rope = q_nope_norm.shape[1], q_pe_norm.shape[1]
    cc, taps = conv_w.shape[2], conv_w.shape[1]
    nh = w_uk.shape[2] // nope
    vh = w_uv.shape[2] // nh
    assert nope + 2 * rope == LANES and 4 * rope == LANES and 2 * vh == LANES and nh % 2 == 0
    assert taps - 1 <= CONV_HALO
    scale = LOG2E / math.sqrt(nope + rope)
    row = lambda v: v.reshape(1, -1).astype(F32)

    wi = w_in[i]
    w_q, w_kv, w_kpe, w_a, w_g = jnp.split(wi, [ql, ql + kl, ql + kl + rope, ql + kl + rope + cc], axis=1)
    w_kpe_sw = _swap_halves(w_kpe, 1)
    w_in_cat = jnp.concatenate([w_q, w_kv, w_a, w_g, w_kpe, w_kpe_sw, w_kpe, w_kpe_sw], axis=1).astype(BF16)

    wq3 = w_uq[i].reshape(ql, nh, nope + rope)
    wq_pe = wq3[:, :, nope:]
    w_uq_cat = jnp.concatenate([wq3[:, :, :nope], wq_pe, _swap_halves(wq_pe, 2)], axis=2)
    w_uq_cat = w_uq_cat.reshape(ql, nh * LANES).astype(BF16)
    gq = jnp.concatenate([q_nope_norm[i], q_pe_norm[i], _swap_halves(q_pe_norm[i], 0)]) * scale
    g_qvec = row(jnp.tile(gq, nh))

    wk3 = w_uk[i].reshape(kl, nh, nope)
    w_uk_pad = jnp.concatenate([wk3, jnp.zeros((kl, nh, LANES - nope), F32)], axis=2).reshape(kl, nh * LANES)
    gk = jnp.concatenate([k_nope_norm[i], jnp.zeros((LANES - nope,), F32)])
    g_kvec = row(jnp.tile(gk, nh))
    g_kpe = row(jnp.tile(jnp.concatenate([k_pe_norm[i], _swap_halves(k_pe_norm[i], 0)]), 2))

    li = jnp.arange(LANES)
    blk_q = (jnp.where((li[:, None] < nope) & (li[None, :] < nope), 1.0 / nope, 0.0)
             + jnp.where((li[:, None] >= nope) & (li[:, None] < nope + rope) & (li[None, :] >= nope),
                         1.0 / rope, 0.0))
    blk_k = jnp.where((li[:, None] < nope) & (li[None, :] < nope), 1.0 / nope, 0.0)
    eye2 = jnp.eye(MXU_DIM // LANES, dtype=F32)
    s_q = jnp.kron(eye2, blk_q).astype(BF16)
    s_k = jnp.kron(eye2, blk_k).astype(BF16)
    s_kpe = jnp.where(li[:, None] < rope, 1.0 / rope, 0.0) * jnp.ones((1, LANES), F32)
    s_kpe = s_kpe.astype(BF16)

    w_ukt_pad = jnp.concatenate([jnp.transpose(wk3, (1, 2, 0)),
                                 jnp.zeros((nh, LANES - nope, kl), F32)], axis=1).astype(BF16)
    w_ukt_perm = jnp.transpose(wk3, (2, 1, 0)).reshape(nope * nh, kl).astype(BF16)
    src = jnp.arange(nh * LANES)
    lane_in = src % LANES
    dst = (src // LANES) * rope + (lane_in - nope) % rope
    p_mat = jnp.where((lane_in >= nope)[:, None] & (dst[:, None] == jnp.arange(nh * rope)[None, :]), 1.0, 0.0)
    p_mat = p_mat.astype(BF16)
    wv3 = w_uv[i].reshape(kl, nh, vh)
    w_bd = jnp.einsum("chv,hg->hcgv", wv3, jnp.eye(nh, dtype=F32)).reshape(nh * kl, nh * vh).astype(BF16)

    mw = norm_out_attn.shape[1]
    return dict(
        nh=nh, ql=ql, kl=kl, cc=cc, rope=rope, nope=nope, vh=vh, vw=nh * vh, taps=taps,
        g_mix=row(norm_mix[i]), w_in=w_in_cat, g_ql=row(q_lora_norm[i]), w_uq=w_uq_cat, g_qvec=g_qvec, s_q=s_q,
        g_kv=row(kv_norm[i]), w_uk_pad=w_uk_pad.astype(BF16), w_uvt=w_uv[i].T.astype(BF16), g_kvec=g_kvec, s_k=s_k, g_kpe=g_kpe, s_kpe=s_kpe,
        w_ukt_pad=w_ukt_pad, w_ukt_perm=w_ukt_perm, p_mat=p_mat, w_bd=w_bd,
        conv_w=conv_w[i].astype(F32), conv_b=row(conv_b[i]), ln_g=row(conv_ln_g[i]), ln_b=row(conv_ln_b[i]),
        g_oc=row(norm_out_conv[i]), g_oa=row(norm_out_attn[i]),
        w_oa=w_out[i][:mw].astype(BF16), w_oc=w_out[i][mw:].astype(BF16),
        g_ffn=row(norm_ffn[i]), w_gate=w_gate[i].astype(BF16), w_up=w_up[i].astype(BF16),
        w_down=w_down[i].astype(BF16), g_ple=row(norm_ple[i]), w_pg=w_ple_gate[i].astype(BF16),
        w_pp=w_ple_proj[i].astype(BF16),
    )


def kernel(x_prompt, x_sample, cache_ckv, cache_kpe, state_conv, page_table, p_prompt, p_sample, norm_mix, w_in, q_lora_norm, w_uq, kv_norm, w_uk, w_uv, q_nope_norm, k_nope_norm, q_pe_norm, k_pe_norm, conv_w, conv_b, conv_ln_g, conv_ln_b, norm_out_attn, norm_out_conv, w_out, norm_ffn, w_gate, w_up, w_down, norm_ple, w_ple_gate, w_ple_proj):
    B, T, D = x_prompt.shape
    Bd, Tq, _ = x_sample.shape
    depth = w_in.shape[0]
    past_len = page_table.shape[1] * cache_ckv.shape[2]
    rope = q_pe_norm.shape[1]
    taps = conv_w.shape[1]
    tab_p = _rope_table(np.arange(T), rope)
    tab_s = np.tile(_rope_table(past_len + np.arange(Tq), rope), (Bd, 1))

    xp, xs = x_prompt, x_sample.reshape(1, Bd * Tq, D)
    outs = [[] for _ in range(6)]
    for i in range(depth):
        p = _layer_params(i, norm_mix, w_in, q_lora_norm, w_uq, kv_norm, w_uk, w_uv, q_nope_norm, k_nope_norm,
                          q_pe_norm, k_pe_norm, conv_w, conv_b, conv_ln_g, conv_ln_b, norm_out_attn,
                          norm_out_conv, w_out, norm_ffn, w_gate, w_up, w_down, norm_ple, w_ple_gate, w_ple_proj)
        nh, kl, cc = p["nh"], p["kl"], p["cc"]
        q, k, vt, ckv, kpe, glu = _in_proj(xp, tab_p, p, BF16)
        attn = _flash(q, k, vt, p["vh"])
        xp = _post(xp, attn, p_prompt[i], glu, p, sample=False)
        outs[0].append(ckv)
        outs[1].append(kpe)
        outs[2].append(glu[:, T - (taps - 1):, :])
        q_s, _, _, ckv_s, kpe_s, glu_s = _in_proj(xs, tab_s, p, F32)
        a_s, qpe_s = _absorb(q_s[0], p)
        ckv_s = ckv_s.reshape(Bd, Tq, kl)
        kpe_s = kpe_s.reshape(Bd, Tq, rope)
        page = cache_kpe.shape[2]
        knewt = jnp.pad(jnp.swapaxes(kpe_s, 1, 2), ((0, 0), (0, 0), (0, page - Tq)))
        o_lat = _paged_attn(page_table, a_s.reshape(Bd, Tq * nh, kl), qpe_s.reshape(Bd, Tq * nh, rope),
                            ckv_s, knewt, cache_ckv, jnp.swapaxes(cache_kpe, 2, 3), p, i)
        attn_s = _sproj(o_lat.reshape(Bd * Tq, nh * kl), p["w_bd"])
        xc = jnp.concatenate([state_conv[i], glu_s.reshape(Bd, Tq, cc)], axis=1)
        xs = _post(xs, attn_s[None], p_sample[i].reshape(1, Bd * Tq, -1), xc, p, sample=True)
        outs[3].append(ckv_s)
        outs[4].append(kpe_s)
        outs[5].append(xc[:, Tq:, :])
    return (xp, xs.reshape(Bd, Tq, D), jnp.stack(outs[0]), jnp.stack(outs[1]), jnp.stack(outs[2]),
            jnp.stack(outs[3]), jnp.stack(outs[4]), jnp.stack(outs[5]))
```

```python
import functools
import math

import jax
import jax.numpy as jnp
import numpy as np
from jax import lax
from jax.experimental import pallas as pl
from jax.experimental.pallas import tpu as pltpu

F32 = jnp.float32
BF16 = jnp.bfloat16
EPS = 1e-6
ROPE_THETA = 10000.0
LOG2E = math.log2(math.e)
NEG_INF = -1e30

LANES = 128
SUBLANES = 8
MXU_DIM = 256
BF16_ROWS = 16
VMEM_LIMIT = 56 * 1024 * 1024
TOKEN_TILE = 512
FLASH_Q_BLOCKS = 4
POST_ROW_GROUPS = 2
CONV_ROWS = 64
CONV_HALO = 32
PAGES_PER_CHUNK = 4
PAGED_GROUP = 4
PAGED_SLOTS = 3


def _rms(x, g):
    return x * lax.rsqrt(jnp.mean(x * x, axis=-1, keepdims=True) + EPS) * g


def _dot(a, b):
    return jnp.dot(a, b, preferred_element_type=F32)


def _dot_t(a, b):
    return lax.dot_general(a, b, (((1,), (1,)), ((), ())), preferred_element_type=F32)


def _seg_ms(x, s_ref):
    x2 = (x * x).astype(BF16)
    w = s_ref.shape[0]
    parts = [_dot(x2[:, j * w:(j + 1) * w], s_ref[...]) for j in range(x.shape[1] // w)]
    return parts[0] if len(parts) == 1 else jnp.concatenate(parts, axis=1)


def _in_body(x_ref, tab_ref, gmix_ref, win_ref, gql_ref, wuq_ref, gqv_ref, sq_ref,
             gkv_ref, wuk_ref, wuvt_ref, gkv_vec_ref, sk_ref, gkpe_ref, skpe_ref,
             q_ref, k_ref, vt_ref, ckv_ref, kpe_ref, glu_ref, *, ql, kl, cc, nh, rope):
    x = x_ref[0]
    h = _rms(x, gmix_ref[...]).astype(BF16)
    z = _dot(h, win_ref[...])
    o_a = ql + kl
    o_g = o_a + cc
    o_k = o_g + cc
    glu_ref[0] = z[:, o_a:o_g] * jax.nn.sigmoid(z[:, o_g:o_k])

    cqn = _rms(z[:, :ql], gql_ref[...]).astype(BF16)
    qa = _dot(cqn, wuq_ref[...])
    tab = tab_ref[...]
    lane = lax.broadcasted_iota(jnp.int32, tab.shape, 1)
    tq = jnp.concatenate([jnp.where(lane >= LANES - 2 * rope, tab, 1.0)] * nh, axis=1)
    q = qa * lax.rsqrt(_seg_ms(qa, sq_ref) + EPS) * gqv_ref[...] * tq
    q_ref[0] = q.astype(q_ref.dtype)

    ckvn = _rms(z[:, ql:o_a], gkv_ref[...])
    ckv_ref[0] = ckvn
    cb = ckvn.astype(BF16)
    vt_ref[0, 0] = _dot_t(wuvt_ref[...], cb).astype(vt_ref.dtype)
    ka = _dot(cb, wuk_ref[...])
    kn = ka * lax.rsqrt(_seg_ms(ka, sk_ref) + EPS) * gkv_vec_ref[...]

    kp = z[:, o_k:o_k + LANES]
    t = kp * lax.rsqrt(_seg_ms(kp, skpe_ref) + EPS) * gkpe_ref[...] * tab
    kd = t + pltpu.roll(t, rope, axis=1)
    kpe_ref[0] = kd[:, :rope]
    kdm = jnp.where(lane >= LANES - 2 * rope, kd, 0.0)
    k_ref[0] = (kn + jnp.concatenate([kdm] * nh, axis=1)).astype(k_ref.dtype)


def _in_proj(x, tab, p, q_dtype):
    B, T, D = x.shape
    tm = min(TOKEN_TILE, T)
    nh, ql, kl, cc, rope, vw = p["nh"], p["ql"], p["kl"], p["cc"], p["rope"], p["vw"]
    const = lambda a: pl.BlockSpec(a.shape, lambda b, i: (0,) * a.ndim, pipeline_mode=pl.Buffered(1))
    tok = lambda w: pl.BlockSpec((1, tm, w), lambda b, i: (b, i, 0))
    tab_spec = pl.BlockSpec((tm, LANES), lambda b, i: (i, 0))
    consts = [p["g_mix"], p["w_in"], p["g_ql"], p["w_uq"], p["g_qvec"], p["s_q"],
              p["g_kv"], p["w_uk_pad"], p["w_uvt"], p["g_kvec"], p["s_k"], p["g_kpe"], p["s_kpe"]]
    out_shape = (
        jax.ShapeDtypeStruct((B, T, nh * LANES), q_dtype),
        jax.ShapeDtypeStruct((B, T, nh * LANES), BF16),
        jax.ShapeDtypeStruct((B, T // tm, vw, tm), BF16),
        jax.ShapeDtypeStruct((B, T, kl), F32),
        jax.ShapeDtypeStruct((B, T, rope), F32),
        jax.ShapeDtypeStruct((B, T, cc), F32),
    )
    return pl.pallas_call(
        functools.partial(_in_body, ql=ql, kl=kl, cc=cc, nh=nh, rope=rope),
        grid=(B, T // tm),
        in_specs=[tok(D), tab_spec] + [const(a) for a in consts],
        out_specs=[tok(nh * LANES), tok(nh * LANES), pl.BlockSpec((1, 1, vw, tm), lambda b, i: (b, i, 0, 0)), tok(kl), tok(rope), tok(cc)],
        out_shape=out_shape,
        compiler_params=pltpu.CompilerParams(
            dimension_semantics=("parallel", "parallel"), vmem_limit_bytes=VMEM_LIMIT),
        name="in_proj",
    )(x, tab, *consts)


def _flash_body(q_ref, k_ref, vt_ref, o_ref, s_ref, mc_ref, m_ref, acc_ref, *, tk, tq, vh):
    i = pl.program_id(2)
    ns = tq // MXU_DIM
    nkb = tq // tk
    m_ref[...] = jnp.full(m_ref.shape, NEG_INF, F32)
    acc_ref[...] = jnp.zeros(acc_ref.shape, F32)
    ones = jnp.ones((acc_ref.shape[1] - vh, tk), BF16)

    def scores(h, j, strips):
        k = k_ref[0, pl.ds(pl.multiple_of(j * tk, tk), tk), h * LANES:(h + 1) * LANES]
        for s in strips:
            sl = slice(s * MXU_DIM, (s + 1) * MXU_DIM)
            st = _dot_t(k, q_ref[0, sl, h * LANES:(h + 1) * LANES])
            s_ref[h, :, sl] = st
            mc_ref[h, :, sl] = jnp.max(st, axis=0, keepdims=True)

    def softmax_pv(h, j, s, key_off):
        sl = slice(s * MXU_DIM, (s + 1) * MXU_DIM)
        st = s_ref[h, :, sl]
        if key_off is not None:
            key = lax.broadcasted_iota(jnp.int32, st.shape, 0) + key_off
            qry = lax.broadcasted_iota(jnp.int32, st.shape, 1) + s * MXU_DIM
            st = jnp.where(key <= qry, st, NEG_INF)
            m_cur = jnp.max(st, axis=0, keepdims=True)
        else:
            m_cur = mc_ref[h, :, sl]
        m_prev = m_ref[h, :, sl]
        m_new = jnp.maximum(m_prev, m_cur)
        alpha = jnp.exp2(m_prev - m_new)
        pt = jnp.exp2((st - m_new).astype(BF16))
        vt1 = jnp.concatenate([vt_ref[0, j, h * vh:(h + 1) * vh, :], ones], axis=0)
        acc_ref[h, :, sl] = alpha * acc_ref[h, :, sl] + _dot(vt1, pt)
        m_ref[h, :, sl] = m_new

    every = range(ns)
    scores(0, 0, every)

    def block(j, carry):
        scores(1, j, every)
        for s in every:
            softmax_pv(0, j, s, None)
        scores(0, j + 1, every)
        for s in every:
            softmax_pv(1, j, s, None)
        return carry

    j0 = nkb * i
    lax.fori_loop(0, j0, block, 0)
    for kb in range(nkb):
        act = [s for s in every if (s + 1) * MXU_DIM > kb * tk]
        off = lambda s, kb=kb: kb * tk if s * MXU_DIM < (kb + 1) * tk else None
        scores(1, j0 + kb, act)
        for s in act:
            softmax_pv(0, j0 + kb, s, off(s))
        if kb + 1 < nkb:
            scores(0, j0 + kb + 1, [s for s in every if (s + 1) * MXU_DIM > (kb + 1) * tk])
        for s in act:
            softmax_pv(1, j0 + kb, s, off(s))
    o_t = jnp.concatenate([acc_ref[h, :vh, :] / acc_ref[h, vh:vh + 1, :] for h in range(2)], axis=0)
    o_ref[0] = o_t.T


def _flash(q, k, vt, vh):
    B, T, W = q.shape
    nhp = W // (2 * LANES)
    tk = vt.shape[3]
    tq = math.gcd(T, FLASH_Q_BLOCKS * tk)
    return pl.pallas_call(
        functools.partial(_flash_body, tk=tk, tq=tq, vh=vh),
        grid=(B, nhp, T // tq),
        in_specs=[
            pl.BlockSpec((1, tq, 2 * LANES), lambda b, hp, i: (b, i, hp)),
            pl.BlockSpec((1, T, 2 * LANES), lambda b, hp, i: (b, 0, hp)),
            pl.BlockSpec((1, T // tk, 2 * vh, tk), lambda b, hp, i: (b, 0, hp, 0)),
        ],
        out_specs=pl.BlockSpec((1, tq, 2 * vh), lambda b, hp, i: (b, i, hp)),
        out_shape=jax.ShapeDtypeStruct((B, T, nhp * 2 * vh), F32),
        scratch_shapes=[pltpu.VMEM((2, tk, tq), F32), pltpu.VMEM((2, 1, tq), F32),
                        pltpu.VMEM((2, 1, tq), F32), pltpu.VMEM((2, vh + BF16_ROWS, tq), F32)],
        compiler_params=pltpu.CompilerParams(
            dimension_semantics=("parallel", "parallel", "arbitrary"), vmem_limit_bytes=VMEM_LIMIT),
        name="flash",
    )(q, k, vt)


def _absorb_body(q_ref, gk_ref, wukt_ref, pmat_ref, a_ref, qpe_ref, *, nh, kl):
    q = q_ref[...]
    qpe_ref[...] = _dot(q.astype(BF16), pmat_ref[...]).astype(qpe_ref.dtype)
    qg = (q * gk_ref[...]).astype(BF16)
    for h in range(nh):
        a_ref[:, h * kl:(h + 1) * kl] = _dot(qg[:, h * LANES:(h + 1) * LANES], wukt_ref[h]).astype(a_ref.dtype)


def _absorb(q, p):
    N, W = q.shape
    nh, kl, rope = p["nh"], p["kl"], p["rope"]
    tm = min(256, N)
    const = lambda a: pl.BlockSpec(a.shape, lambda i: (0,) * a.ndim, pipeline_mode=pl.Buffered(1))
    consts = [p["g_kvec"], p["w_ukt_pad"], p["p_mat"]]
    return pl.pallas_call(
        functools.partial(_absorb_body, nh=nh, kl=kl),
        grid=(N // tm,),
        in_specs=[pl.BlockSpec((tm, W), lambda i: (i, 0))] + [const(a) for a in consts],
        out_specs=[pl.BlockSpec((tm, nh * kl), lambda i: (i, 0)),
                   pl.BlockSpec((tm, nh * rope), lambda i: (i, 0))],
        out_shape=(jax.ShapeDtypeStruct((N, nh * kl), BF16), jax.ShapeDtypeStruct((N, nh * rope), BF16)),
        compiler_params=pltpu.CompilerParams(dimension_semantics=("parallel",), vmem_limit_bytes=VMEM_LIMIT),
        name="absorb",
    )(q, *consts)


def _paged_body(pt_ref, a_ref, qpe_ref, wukt_ref, cnew_ref, knewt_ref, cache_c_ref, cache_kt_ref, o_ref,
                cbuf, kbuf, sem, s_ref, cb_ref, m_ref, l_ref, acc_ref, *, layer, n_pages, page, nh, nope):
    g = pl.program_id(0)
    ng = pl.num_programs(0)
    grp, rows, _ = a_ref.shape
    ch = PAGES_PER_CHUNK
    n_chunks = n_pages // ch

    def copies(gg, cc, slot):
        out = []
        for e in range(grp):
            for i in range(ch):
                pg = pt_ref[(gg * grp + e) * n_pages + cc * ch + i]
                out.append(pltpu.make_async_copy(cache_c_ref.at[layer, pg],
                                                 cbuf.at[slot, e, pl.ds(i * page, page)], sem.at[slot, 0]))
                out.append(pltpu.make_async_copy(cache_kt_ref.at[layer, pg],
                                                 kbuf.at[slot, e, :, pl.ds(i * page, page)], sem.at[slot, 1]))
        return out

    def fetch(gg, cc, slot):
        for n, c in enumerate(copies(gg, cc, slot)):
            c.start(priority=(n // 2 + n) % 2)

    nslot = cbuf.shape[0]
    ahead = nslot - 1

    @pl.when(g == 0)
    def _():
        for cc in range(ahead):
            fetch(0, cc, cc)

    m_ref[...] = jnp.full(m_ref.shape, NEG_INF, F32)
    l_ref[...] = jnp.zeros(l_ref.shape, F32)
    acc_ref[...] = jnp.zeros(acc_ref.shape, F32)

    def scores(e, ckv, kpet):
        cb = ckv.astype(BF16)
        kt = _dot_t(wukt_ref[...], cb)
        ss = jnp.sum((kt * kt).reshape(nope, nh, kt.shape[1]), axis=0)
        r = lax.rsqrt(ss * (1.0 / nope) + EPS)
        s = (_dot_t(a_ref[e], cb) * jnp.concatenate([r] * (rows // nh), axis=0)
             + _dot(qpe_ref[e], kpet.astype(BF16)))
        return s, cb

    def softmax_pv(e, s, cb):
        m_prev = m_ref[e]
        m_new = jnp.maximum(m_prev, jnp.max(s, axis=-1, keepdims=True))
        alpha = jnp.exp2(m_prev - m_new)
        pr = jnp.exp2(s - m_new)
        l_ref[e] = alpha * l_ref[e] + jnp.sum(pr, axis=-1, keepdims=True)
        acc_ref[e] = alpha * acc_ref[e] + _dot(pr.astype(BF16), cb)
        m_ref[e] = m_new

    def stage_scores(e, slot):
        s, cb = scores(e, cbuf[slot, e], kbuf[slot, e])
        s_ref[e] = s
        cb_ref[e] = cb

    def stage_softmax(e):
        softmax_pv(e, s_ref[e], cb_ref[e])

    def chunk(c, first):
        seq = g * n_chunks + c
        slot = seq % nslot
        nxt = c + ahead
        wrap = nxt >= n_chunks
        fetch(jnp.where(wrap, jnp.minimum(g + 1, ng - 1), g), jnp.where(wrap, nxt - n_chunks, nxt),
              (seq + ahead) % nslot)
        for cp in copies(g, c, slot):
            cp.wait()
        for e in range(grp):
            stage_scores(e, slot)
            if e > 0 or not first:
                stage_softmax((e - 1) % grp)

    chunk(0, True)

    def later_chunk(c, carry):
        chunk(c, False)
        return carry

    lax.fori_loop(1, n_chunks, later_chunk, 0)
    stage_softmax(grp - 1)

    @pl.when(g == ng - 1)
    def _():
        for cc in range(ahead):
            for cp in copies(g, cc, (ng * n_chunks + cc) % nslot):
                cp.wait()

    tq = cnew_ref.shape[1]
    zpad = jnp.zeros((page - tq, cnew_ref.shape[2]), F32)
    qi = lax.broadcasted_iota(jnp.int32, (rows, page), 0) // nh
    kj = lax.broadcasted_iota(jnp.int32, (rows, page), 1)
    for e in range(grp):
        s, cb = scores(e, jnp.concatenate([cnew_ref[e], zpad], axis=0), knewt_ref[e])
        softmax_pv(e, jnp.where(kj <= qi, s, NEG_INF), cb)
        o_ref[e] = acc_ref[e] / l_ref[e]


def _paged_attn(page_table, a, qpe, cnew, knewt, cache_c, cache_kt, p, layer):
    Bd, rows, kl = a.shape
    nh, rope, nope = p["nh"], p["rope"], p["nope"]
    n_pages = page_table.shape[1]
    page = cache_c.shape[2]
    grp = math.gcd(PAGED_GROUP, Bd)
    assert n_pages % PAGES_PER_CHUNK == 0 and n_pages // PAGES_PER_CHUNK >= PAGED_SLOTS
    keys = PAGES_PER_CHUNK * page
    tq = cnew.shape[1]
    row_blk = lambda r, w: pl.BlockSpec((grp, r, w), lambda b, pt: (b, 0, 0))
    grid_spec = pltpu.PrefetchScalarGridSpec(
        num_scalar_prefetch=1,
        grid=(Bd // grp,),
        in_specs=[
            row_blk(rows, kl),
            row_blk(rows, rope),
            pl.BlockSpec(p["w_ukt_perm"].shape, lambda b, pt: (0, 0), pipeline_mode=pl.Buffered(1)),
            row_blk(tq, kl),
            row_blk(rope, page),
            pl.BlockSpec(memory_space=pl.ANY),
            pl.BlockSpec(memory_space=pl.ANY),
        ],
        out_specs=row_blk(rows, kl),
        scratch_shapes=[
            pltpu.VMEM((PAGED_SLOTS, grp, keys, kl), F32),
            pltpu.VMEM((PAGED_SLOTS, grp, rope, keys), F32),
            pltpu.SemaphoreType.DMA((PAGED_SLOTS, 2)),
            pltpu.VMEM((grp, rows, keys), F32),
            pltpu.VMEM((grp, keys, kl), BF16),
            pltpu.VMEM((grp, rows, 1), F32),
            pltpu.VMEM((grp, rows, 1), F32),
            pltpu.VMEM((grp, rows, kl), F32),
        ],
    )
    return pl.pallas_call(
        functools.partial(_paged_body, layer=layer, n_pages=n_pages, page=page, nh=nh, nope=nope),
        grid_spec=grid_spec,
        out_shape=jax.ShapeDtypeStruct((Bd, rows, kl), F32),
        compiler_params=pltpu.CompilerParams(dimension_semantics=("arbitrary",), vmem_limit_bytes=VMEM_LIMIT),
        name="paged_attn",
    )(page_table.reshape(-1), a, qpe, p["w_ukt_perm"], cnew, knewt, cache_c, cache_kt)


def _sproj_body(o_ref, w_ref, out_ref):
    out_ref[...] = _dot(o_ref[...].astype(BF16), w_ref[...])


def _sproj(o_flat, w_bd):
    N, K = o_flat.shape
    tm = min(TOKEN_TILE, N)
    return pl.pallas_call(
        _sproj_body,
        grid=(N // tm,),
        in_specs=[pl.BlockSpec((tm, K), lambda i: (i, 0)),
                  pl.BlockSpec(w_bd.shape, lambda i: (0, 0), pipeline_mode=pl.Buffered(1))],
        out_specs=pl.BlockSpec((tm, w_bd.shape[1]), lambda i: (i, 0)),
        out_shape=jax.ShapeDtypeStruct((N, w_bd.shape[1]), F32),
        compiler_params=pltpu.CompilerParams(dimension_semantics=("parallel",), vmem_limit_bytes=VMEM_LIMIT),
        name="sproj",
    )(o_flat, w_bd)


def _post_tail(y, x, attn, pe, refs, out_ref):
    (lng_ref, lnb_ref, goc_ref, goa_ref, woa_ref, woc_ref, gffn_ref, wg_ref, wu_ref, wd_ref,
     gple_ref, wpg_ref, wpp_ref) = refs
    n = y.shape[0]
    ng = POST_ROW_GROUPS if n % (POST_ROW_GROUPS * BF16_ROWS) == 0 else 1
    rows = [slice(r * (n // ng), (r + 1) * (n // ng)) for r in range(ng)]

    def mix(r):
        yr = y[r]
        mu = jnp.mean(yr, axis=-1, keepdims=True)
        yc = yr - mu
        yn = yc * lax.rsqrt(jnp.mean(yc * yc, axis=-1, keepdims=True) + EPS) * lng_ref[...] + lnb_ref[...]
        conv = yn * jax.nn.sigmoid(yn)
        x1 = x[r] + _dot(_rms(attn[r], goa_ref[...]).astype(BF16), woa_ref[...])
        return x1 + _dot(_rms(conv, goc_ref[...]).astype(BF16), woc_ref[...])

    def ffn_in(x1):
        h = _rms(x1, gffn_ref[...]).astype(BF16)
        return _dot(h, wg_ref[...]), _dot(h, wu_ref[...])

    def ffn_out(x1, gu):
        g, u = gu
        return x1 + _dot((g * jax.nn.sigmoid(g) * u).astype(BF16), wd_ref[...])

    def ple(r, x2):
        gate = jax.nn.sigmoid(_dot(_rms(x2, gple_ref[...]).astype(BF16), wpg_ref[...]))
        out_ref[0, r, :] = x2 + gate * _dot(pe[r].astype(BF16), wpp_ref[...])

    x1 = [mix(r) for r in rows]
    gu = [ffn_in(a) for a in x1]
    x2 = [ffn_out(a, b) for a, b in zip(x1, gu)]
    for r, a in zip(rows, x2):
        ple(r, a)


def _post_prompt_body(x_ref, attn_ref, pe_ref, glu_ref, prev_ref, cw_ref, cb_ref, *rest, taps):
    refs, out_ref, xs_ref, sh_ref, y_ref = rest[:-4], rest[-4], rest[-3], rest[-2], rest[-1]
    i = pl.program_id(1)
    tm = glu_ref.shape[1]
    xs_ref[0:CONV_HALO, :] = jnp.where(i > 0, prev_ref[0], 0.0)
    xs_ref[CONV_HALO:, :] = glu_ref[0]
    off = CONV_HALO - (taps - 1)
    span = sh_ref.shape[1] - SUBLANES * ((taps - 1) // SUBLANES)
    for h0 in range(0, tm, span):
        for b in range(min(SUBLANES, taps)):
            nrow = span + SUBLANES * ((taps - 1 - b) // SUBLANES)
            sh_ref[b, 0:nrow, :] = xs_ref[h0 + off + b:h0 + off + b + nrow, :]
        for r0 in range(0, span, CONV_ROWS):
            acc = jnp.broadcast_to(cb_ref[...], (CONV_ROWS, cb_ref.shape[1]))
            for k in range(taps):
                a, b = divmod(k, SUBLANES)
                acc = acc + cw_ref[k:k + 1, :] * sh_ref[b, r0 + SUBLANES * a:r0 + SUBLANES * a + CONV_ROWS, :]
            y_ref[h0 + r0:h0 + r0 + CONV_ROWS, :] = acc
    _post_tail(y_ref[...], x_ref[0], attn_ref[0], pe_ref[0], refs, out_ref)


def _post_sample_body(x_ref, attn_ref, pe_ref, xc_ref, cw_ref, cb_ref, *rest, taps):
    refs, out_ref = rest[:-1], rest[-1]
    nb, _, cc = xc_ref.shape
    tq = x_ref.shape[1] // nb
    acc = jnp.broadcast_to(cb_ref[...].reshape(1, 1, cc), (nb, tq, cc))
    for k in range(taps):
        acc = acc + cw_ref[k:k + 1, :].reshape(1, 1, cc) * xc_ref[:, k:k + tq, :]
    _post_tail(acc.reshape(nb * tq, cc), x_ref[0], attn_ref[0], pe_ref[0], refs, out_ref)


def _post(x, attn, pe, conv_in, p, sample):
    B, T, D = x.shape
    tm = min(TOKEN_TILE, T)
    cc, taps = p["cc"], p["taps"]
    const = lambda a: pl.BlockSpec(a.shape, lambda b, i: (0,) * a.ndim, pipeline_mode=pl.Buffered(1))
    tok = lambda w: pl.BlockSpec((1, tm, w), lambda b, i: (b, i, 0))
    consts = [p["conv_w"], p["conv_b"], p["ln_g"], p["ln_b"], p["g_oc"], p["g_oa"], p["w_oa"], p["w_oc"],
              p["g_ffn"], p["w_gate"], p["w_up"], p["w_down"], p["g_ple"], p["w_pg"], p["w_pp"]]
    if sample:
        tq = conv_in.shape[1] - (taps - 1)
        nb = tm // tq
        conv_specs = [pl.BlockSpec((nb,) + conv_in.shape[1:], lambda b, i: (i, 0, 0))]
        conv_args = [conv_in]
        body = functools.partial(_post_sample_body, taps=taps)
        scratch = []
    else:
        per = tm // CONV_HALO
        conv_specs = [tok(cc),
                      pl.BlockSpec((1, CONV_HALO, cc), lambda b, i: (b, jnp.maximum(i * per - 1, 0), 0))]
        conv_args = [conv_in, conv_in]
        body = functools.partial(_post_prompt_body, taps=taps)
        span = max(CONV_ROWS, tm // 2)
        scratch = [pltpu.VMEM((tm + CONV_HALO, cc), F32),
                   pltpu.VMEM((SUBLANES, span + SUBLANES * ((taps - 1) // SUBLANES), cc), F32),
                   pltpu.VMEM((tm, cc), F32)]
    return pl.pallas_call(
        body,
        grid=(B, T // tm),
        in_specs=[tok(D), tok(attn.shape[2]), tok(pe.shape[2])] + conv_specs + [const(a) for a in consts],
        out_specs=tok(D),
        out_shape=jax.ShapeDtypeStruct((B, T, D), F32),
        scratch_shapes=scratch,
        compiler_params=pltpu.CompilerParams(
            dimension_semantics=("parallel", "parallel"), vmem_limit_bytes=VMEM_LIMIT),
        name="post_sample" if sample else "post_prompt",
    )(x, attn, pe, *conv_args, *consts)


def _rope_table(pos, rope):
    inv = 1.0 / (ROPE_THETA ** (np.arange(0, rope, 2, dtype=np.float64) / rope))
    ang = np.asarray(pos, np.float64)[:, None] * inv[None, :]
    cos, sin = np.cos(ang), np.sin(ang)
    cos2 = np.concatenate([cos, cos], axis=1)
    sin2 = np.concatenate([-sin, sin], axis=1)
    return np.concatenate([cos2, sin2, cos2, sin2], axis=1).astype(np.float32)


def _swap_halves(w, axis):
    a, b = jnp.split(w, 2, axis=axis)
    return jnp.concatenate([b, a], axis=axis)


def _layer_params(i, norm_mix, w_in, q_lora_norm, w_uq, kv_norm, w_uk, w_uv, q_nope_norm, k_nope_norm,
                  q_pe_norm, k_pe_norm, conv_w, conv_b, conv_ln_g, conv_ln_b, norm_out_attn, norm_out_conv,
                  w_out, norm_ffn, w_gate, w_up, w_down, norm_ple, w_ple_gate, w_ple_proj):
    ql, kl = q_lora_norm.shape[1], kv_norm.shape[1]
    nope, rope = q_nope_norm.shape[1], q_pe_norm.shape[1]
    cc, taps = conv_w.shape[2], conv_w.shape[1]
    nh = w_uk.shape[2] // nope
    vh = w_uv.shape[2] // nh
    assert nope + 2 * rope == LANES and 4 * rope == LANES and 2 * vh == LANES and nh % 2 == 0
    assert taps - 1 <= CONV_HALO
    scale = LOG2E / math.sqrt(nope + rope)
    row = lambda v: v.reshape(1, -1).astype(F32)

    wi = w_in[i]
    w_q, w_kv, w_kpe, w_a, w_g = jnp.split(wi, [ql, ql + kl, ql + kl + rope, ql + kl + rope + cc], axis=1)
    w_kpe_sw = _swap_halves(w_kpe, 1)
    w_in_cat = jnp.concatenate([w_q, w_kv, w_a, w_g, w_kpe, w_kpe_sw, w_kpe, w_kpe_sw], axis=1).astype(BF16)

    wq3 = w_uq[i].reshape(ql, nh, nope + rope)
    wq_pe = wq3[:, :, nope:]
    w_uq_cat = jnp.concatenate([wq3[:, :, :nope], wq_pe, _swap_halves(wq_pe, 2)], axis=2)
    w_uq_cat = w_uq_cat.reshape(ql, nh * LANES).astype(BF16)
    gq = jnp.concatenate([q_nope_norm[i], q_pe_norm[i], _swap_halves(q_pe_norm[i], 0)]) * scale
    g_qvec = row(jnp.tile(gq, nh))

    wk3 = w_uk[i].reshape(kl, nh, nope)
    w_uk_pad = jnp.concatenate([wk3, jnp.zeros((kl, nh, LANES - nope), F32)], axis=2).reshape(kl, nh * LANES)
    gk = jnp.concatenate([k_nope_norm[i], jnp.zeros((LANES - nope,), F32)])
    g_kvec = row(jnp.tile(gk, nh))
    g_kpe = row(jnp.tile(jnp.concatenate([k_pe_norm[i], _swap_halves(k_pe_norm[i], 0)]), 2))

    li = jnp.arange(LANES)
    blk_q = (jnp.where((li[:, None] < nope) & (li[None, :] < nope), 1.0 / nope, 0.0)
             + jnp.where((li[:, None] >= nope) & (li[:, None] < nope + rope) & (li[None, :] >= nope),
                         1.0 / rope, 0.0))
    blk_k = jnp.where((li[:, None] < nope) & (li[None, :] < nope), 1.0 / nope, 0.0)
    eye2 = jnp.eye(MXU_DIM // LANES, dtype=F32)
    s_q = jnp.kron(eye2, blk_q).astype(BF16)
    s_k = jnp.kron(eye2, blk_k).astype(BF16)
    s_kpe = jnp.where(li[:, None] < rope, 1.0 / rope, 0.0) * jnp.ones((1, LANES), F32)
    s_kpe = s_kpe.astype(BF16)

    w_ukt_pad = jnp.concatenate([jnp.transpose(wk3, (1, 2, 0)),
                                 jnp.zeros((nh, LANES - nope, kl), F32)], axis=1).astype(BF16)
    w_ukt_perm = jnp.transpose(wk3, (2, 1, 0)).reshape(nope * nh, kl).astype(BF16)
    src = jnp.arange(nh * LANES)
    lane_in = src % LANES
    dst = (src // LANES) * rope + (lane_in - nope) % rope
    p_mat = jnp.where((lane_in >= nope)[:, None] & (dst[:, None] == jnp.arange(nh * rope)[None, :]), 1.0, 0.0)
    p_mat = p_mat.astype(BF16)
    wv3 = w_uv[i].reshape(kl, nh, vh)
    w_bd = jnp.einsum("chv,hg->hcgv", wv3, jnp.eye(nh, dtype=F32)).reshape(nh * kl, nh * vh).astype(BF16)

    mw = norm_out_attn.shape[1]
    return dict(
        nh=nh, ql=ql, kl=kl, cc=cc, rope=rope, nope=nope, vh=vh, vw=nh * vh, taps=taps,
        g_mix=row(norm_mix[i]), w_in=w_in_cat, g_ql=row(q_lora_norm[i]), w_uq=w_uq_cat, g_qvec=g_qvec, s_q=s_q,
        g_kv=row(kv_norm[i]), w_uk_pad=w_uk_pad.astype(BF16), w_uvt=w_uv[i].T.astype(BF16), g_kvec=g_kvec, s_k=s_k, g_kpe=g_kpe, s_kpe=s_kpe,
        w_ukt_pad=w_ukt_pad, w_ukt_perm=w_ukt_perm, p_mat=p_mat, w_bd=w_bd,
        conv_w=conv_w[i].astype(F32), conv_b=row(conv_b[i]), ln_g=row(conv_ln_g[i]), ln_b=row(conv_ln_b[i]),
        g_oc=row(norm_out_conv[i]), g_oa=row(norm_out_attn[i]),
        w_oa=w_out[i][:mw].astype(BF16), w_oc=w_out[i][mw:].astype(BF16),
        g_ffn=row(norm_ffn[i]), w_gate=w_gate[i].astype(BF16), w_up=w_up[i].astype(BF16),
        w_down=w_down[i].astype(BF16), g_ple=row(norm_ple[i]), w_pg=w_ple_gate[i].astype(BF16),
        w_pp=w_ple_proj[i].astype(BF16),
    )


def kernel(x_prompt, x_sample, cache_ckv, cache_kpe, state_conv, page_table, p_prompt, p_sample, norm_mix, w_in, q_lora_norm, w_uq, kv_norm, w_uk, w_uv, q_nope_norm, k_nope_norm, q_pe_norm, k_pe_norm, conv_w, conv_b, conv_ln_g, conv_ln_b, norm_out_attn, norm_out_conv, w_out, norm_ffn, w_gate, w_up, w_down, norm_ple, w_ple_gate, w_ple_proj):
    B, T, D = x_prompt.shape
    Bd, Tq, _ = x_sample.shape
    depth = w_in.shape[0]
    past_len = page_table.shape[1] * cache_ckv.shape[2]
    rope = q_pe_norm.shape[1]
    taps = conv_w.shape[1]
    tab_p = _rope_table(np.arange(T), rope)
    tab_s = np.tile(_rope_table(past_len + np.arange(Tq), rope), (Bd, 1))

    xp, xs = x_prompt, x_sample.reshape(1, Bd * Tq, D)
    outs = [[] for _ in range(6)]
    for i in range(depth):
        p = _layer_params(i, norm_mix, w_in, q_lora_norm, w_uq, kv_norm, w_uk, w_uv, q_nope_norm, k_nope_norm,
                          q_pe_norm, k_pe_norm, conv_w, conv_b, conv_ln_g, conv_ln_b, norm_out_attn,
                          norm_out_conv, w_out, norm_ffn, w_gate, w_up, w_down, norm_ple, w_ple_gate, w_ple_proj)
        nh, kl, cc = p["nh"], p["kl"], p["cc"]
        q, k, vt, ckv, kpe, glu = _in_proj(xp, tab_p, p, BF16)
        attn = _flash(q, k, vt, p["vh"])
        xp = _post(xp, attn, p_prompt[i], glu, p, sample=False)
        outs[0].append(ckv)
        outs[1].append(kpe)
        outs[2].append(glu[:, T - (taps - 1):, :])
        q_s, _, _, ckv_s, kpe_s, glu_s = _in_proj(xs, tab_s, p, F32)
        a_s, qpe_s = _absorb(q_s[0], p)
        ckv_s = ckv_s.reshape(Bd, Tq, kl)
        kpe_s = kpe_s.reshape(Bd, Tq, rope)
        page = cache_kpe.shape[2]
        knewt = jnp.pad(jnp.swapaxes(kpe_s, 1, 2), ((0, 0), (0, 0), (0, page - Tq)))
        o_lat = _paged_attn(page_table, a_s.reshape(Bd, Tq * nh, kl), qpe_s.reshape(Bd, Tq * nh, rope),
                            ckv_s, knewt, cache_ckv, jnp.swapaxes(cache_kpe, 2, 3), p, i)
        attn_s = _sproj(o_lat.reshape(Bd * Tq, nh * kl), p["w_bd"])
        xc = jnp.concatenate([state_conv[i], glu_s.reshape(Bd, Tq, cc)], axis=1)
        xs = _post(xs, attn_s[None], p_sample[i].reshape(1, Bd * Tq, -1), xc, p, sample=True)
        outs[3].append(ckv_s)
        outs[4].append(kpe_s)
        outs[5].append(xc[:, Tq:, :])
    return (xp, xs.reshape(Bd, Tq, D), jnp.stack(outs[0]), jnp.stack(outs[1]), jnp.stack(outs[2]),
            jnp.stack(outs[3]), jnp.stack(outs[4]), jnp.stack(outs[5]))
```

```python
import functools
import math

import jax
import jax.numpy as jnp
import numpy as np
from jax import lax
from jax.experimental import pallas as pl
from jax.experimental.pallas import tpu as pltpu

F32 = jnp.float32
BF16 = jnp.bfloat16
EPS = 1e-6
ROPE_THETA = 10000.0
LOG2E = math.log2(math.e)
NEG_INF = -1e30

LANES = 128
SUBLANES = 8
MXU_DIM = 256
BF16_ROWS = 16
VMEM_LIMIT = 56 * 1024 * 1024
TOKEN_TILE = 512
FLASH_Q_BLOCKS = 8
POST_ROW_GROUPS = 2
CONV_ROWS = 64
CONV_HALO = 32
PAGES_PER_CHUNK = 4
PAGED_GROUP = 4
PAGED_SLOTS = 3


def _rms(x, g):
    return x * lax.rsqrt(jnp.mean(x * x, axis=-1, keepdims=True) + EPS) * g


def _dot(a, b):
    return jnp.dot(a, b, preferred_element_type=F32)


def _dot_t(a, b):
    return lax.dot_general(a, b, (((1,), (1,)), ((), ())), preferred_element_type=F32)


def _seg_ms(x, s_ref):
    x2 = (x * x).astype(BF16)
    w = s_ref.shape[0]
    parts = [_dot(x2[:, j * w:(j + 1) * w], s_ref[...]) for j in range(x.shape[1] // w)]
    return parts[0] if len(parts) == 1 else jnp.concatenate(parts, axis=1)


def _in_body(x_ref, tab_ref, gmix_ref, win_ref, gql_ref, wuq_ref, gqv_ref, sq_ref,
             gkv_ref, wuk_ref, wuvt_ref, gkv_vec_ref, sk_ref, gkpe_ref, skpe_ref,
             q_ref, k_ref, vt_ref, ckv_ref, kpe_ref, glu_ref, *, ql, kl, cc, nh, rope):
    x = x_ref[0]
    h = _rms(x, gmix_ref[...]).astype(BF16)
    z = _dot(h, win_ref[...])
    o_a = ql + kl
    o_g = o_a + cc
    o_k = o_g + cc
    glu_ref[0] = z[:, o_a:o_g] * jax.nn.sigmoid(z[:, o_g:o_k])

    cqn = _rms(z[:, :ql], gql_ref[...]).astype(BF16)
    qa = _dot(cqn, wuq_ref[...])
    tab = tab_ref[...]
    lane = lax.broadcasted_iota(jnp.int32, tab.shape, 1)
    tq = jnp.concatenate([jnp.where(lane >= LANES - 2 * rope, tab, 1.0)] * nh, axis=1)
    q = qa * lax.rsqrt(_seg_ms(qa, sq_ref) + EPS) * gqv_ref[...] * tq
    q_ref[0] = q.astype(q_ref.dtype)

    ckvn = _rms(z[:, ql:o_a], gkv_ref[...])
    ckv_ref[0] = ckvn
    cb = ckvn.astype(BF16)
    vt_ref[0, 0] = _dot_t(wuvt_ref[...], cb).astype(vt_ref.dtype)
    ka = _dot(cb, wuk_ref[...])
    kn = ka * lax.rsqrt(_seg_ms(ka, sk_ref) + EPS) * gkv_vec_ref[...]

    kp = z[:, o_k:o_k + LANES]
    t = kp * lax.rsqrt(_seg_ms(kp, skpe_ref) + EPS) * gkpe_ref[...] * tab
    kd = t + pltpu.roll(t, rope, axis=1)
    kpe_ref[0] = kd[:, :rope]
    kdm = jnp.where(lane >= LANES - 2 * rope, kd, 0.0)
    k_ref[0] = (kn + jnp.concatenate([kdm] * nh, axis=1)).astype(k_ref.dtype)


def _in_proj(x, tab, p, q_dtype):
    B, T, D = x.shape
    tm = min(TOKEN_TILE, T)
    nh, ql, kl, cc, rope, vw = p["nh"], p["ql"], p["kl"], p["cc"], p["rope"], p["vw"]
    const = lambda a: pl.BlockSpec(a.shape, lambda b, i: (0,) * a.ndim, pipeline_mode=pl.Buffered(1))
    tok = lambda w: pl.BlockSpec((1, tm, w), lambda b, i: (b, i, 0))
    tab_spec = pl.BlockSpec((tm, LANES), lambda b, i: (i, 0))
    consts = [p["g_mix"], p["w_in"], p["g_ql"], p["w_uq"], p["g_qvec"], p["s_q"],
              p["g_kv"], p["w_uk_pad"], p["w_uvt"], p["g_kvec"], p["s_k"], p["g_kpe"], p["s_kpe"]]
    out_shape = (
        jax.ShapeDtypeStruct((B, T, nh * LANES), q_dtype),
        jax.ShapeDtypeStruct((B, T, nh * LANES), BF16),
        jax.ShapeDtypeStruct((B, T // tm, vw, tm), BF16),
        jax.ShapeDtypeStruct((B, T, kl), F32),
        jax.ShapeDtypeStruct((B, T, rope), F32),
        jax.ShapeDtypeStruct((B, T, cc), F32),
    )
    return pl.pallas_call(
        functools.partial(_in_body, ql=ql, kl=kl, cc=cc, nh=nh, rope=rope),
        grid=(B, T // tm),
        in_specs=[tok(D), tab_spec] + [const(a) for a in consts],
        out_specs=[tok(nh * LANES), tok(nh * LANES), pl.BlockSpec((1, 1, vw, tm), lambda b, i: (b, i, 0, 0)), tok(kl), tok(rope), tok(cc)],
        out_shape=out_shape,
        compiler_params=pltpu.CompilerParams(
            dimension_semantics=("parallel", "parallel"), vmem_limit_bytes=VMEM_LIMIT),
        name="in_proj",
    )(x, tab, *consts)


def _flash_body(q_ref, k_ref, vt_ref, o_ref, s_ref, mc_ref, m_ref, acc_ref, *, tk, tq, vh):
    i = pl.program_id(2)
    ns = tq // MXU_DIM
    nkb = tq // tk
    m_ref[...] = jnp.full(m_ref.shape, NEG_INF, F32)
    acc_ref[...] = jnp.zeros(acc_ref.shape, F32)
    ones = jnp.ones((acc_ref.shape[1] - vh, tk), BF16)

    def scores(h, j, strips):
        k = k_ref[0, pl.ds(pl.multiple_of(j * tk, tk), tk), h * LANES:(h + 1) * LANES]
        for s in strips:
            sl = slice(s * MXU_DIM, (s + 1) * MXU_DIM)
            st = _dot_t(k, q_ref[0, sl, h * LANES:(h + 1) * LANES])
            s_ref[h, :, sl] = st
            mc_ref[h, :, sl] = jnp.max(st, axis=0, keepdims=True)

    def softmax_pv(h, j, s, key_off):
        sl = slice(s * MXU_DIM, (s + 1) * MXU_DIM)
        st = s_ref[h, :, sl]
        if key_off is not None:
            key = lax.broadcasted_iota(jnp.int32, st.shape, 0) + key_off
            qry = lax.broadcasted_iota(jnp.int32, st.shape, 1) + s * MXU_DIM
            st = jnp.where(key <= qry, st, NEG_INF)
            m_cur = jnp.max(st, axis=0, keepdims=True)
        else:
            m_cur = mc_ref[h, :, sl]
        m_prev = m_ref[h, :, sl]
        m_new = jnp.maximum(m_prev, m_cur)
        alpha = jnp.exp2(m_prev - m_new)
        pt = jnp.exp2((st - m_new).astype(BF16))
        vt1 = jnp.concatenate([vt_ref[0, j, h * vh:(h + 1) * vh, :], ones], axis=0)
        acc_ref[h, :, sl] = alpha * acc_ref[h, :, sl] + _dot(vt1, pt)
        m_ref[h, :, sl] = m_new

    every = range(ns)
    scores(0, 0, every)

    def block(j, carry):
        scores(1, j, every)
        for s in every:
            softmax_pv(0, j, s, None)
        scores(0, j + 1, every)
        for s in every:
            softmax_pv(1, j, s, None)
        return carry

    j0 = nkb * i
    lax.fori_loop(0, j0, block, 0)
    for kb in range(nkb):
        act = [s for s in every if (s + 1) * MXU_DIM > kb * tk]
        off = lambda s, kb=kb: kb * tk if s * MXU_DIM < (kb + 1) * tk else None
        scores(1, j0 + kb, act)
        for s in act:
            softmax_pv(0, j0 + kb, s, off(s))
        if kb + 1 < nkb:
            scores(0, j0 + kb + 1, [s for s in every if (s + 1) * MXU_DIM > (kb + 1) * tk])
        for s in act:
            softmax_pv(1, j0 + kb, s, off(s))
    o_t = jnp.concatenate([acc_ref[h, :vh, :] / acc_ref[h, vh:vh + 1, :] for h in range(2)], axis=0)
    o_ref[0] = o_t.T


def _flash(q, k, vt, vh):
    B, T, W = q.shape
    nhp = W // (2 * LANES)
    tk = vt.shape[3]
    tq = math.gcd(T, FLASH_Q_BLOCKS * tk)
    return pl.pallas_call(
        functools.partial(_flash_body, tk=tk, tq=tq, vh=vh),
        grid=(B, nhp, T // tq),
        in_specs=[
            pl.BlockSpec((1, tq, 2 * LANES), lambda b, hp, i: (b, i, hp)),
            pl.BlockSpec((1, T, 2 * LANES), lambda b, hp, i: (b, 0, hp)),
            pl.BlockSpec((1, T // tk, 2 * vh, tk), lambda b, hp, i: (b, 0, hp, 0)),
        ],
        out_specs=pl.BlockSpec((1, tq, 2 * vh), lambda b, hp, i: (b, i, hp)),
        out_shape=jax.ShapeDtypeStruct((B, T, nhp * 2 * vh), F32),
        scratch_shapes=[pltpu.VMEM((2, tk, tq), F32), pltpu.VMEM((2, 1, tq), F32),
                        pltpu.VMEM((2, 1, tq), F32), pltpu.VMEM((2, vh + BF16_ROWS, tq), F32)],
        compiler_params=pltpu.CompilerParams(
            dimension_semantics=("parallel", "parallel", "arbitrary"), vmem_limit_bytes=VMEM_LIMIT),
        name="flash",
    )(q, k, vt)


def _absorb_body(q_ref, gk_ref, wukt_ref, pmat_ref, a_ref, qpe_ref, *, nh, kl):
    q = q_ref[...]
    qpe_ref[...] = _dot(q.astype(BF16), pmat_ref[...]).astype(qpe_ref.dtype)
    qg = (q * gk_ref[...]).astype(BF16)
    for h in range(nh):
        a_ref[:, h * kl:(h + 1) * kl] = _dot(qg[:, h * LANES:(h + 1) * LANES], wukt_ref[h]).astype(a_ref.dtype)


def _absorb(q, p):
    N, W = q.shape
    nh, kl, rope = p["nh"], p["kl"], p["rope"]
    tm = min(256, N)
    const = lambda a: pl.BlockSpec(a.shape, lambda i: (0,) * a.ndim, pipeline_mode=pl.Buffered(1))
    consts = [p["g_kvec"], p["w_ukt_pad"], p["p_mat"]]
    return pl.pallas_call(
        functools.partial(_absorb_body, nh=nh, kl=kl),
        grid=(N // tm,),
        in_specs=[pl.BlockSpec((tm, W), lambda i: (i, 0))] + [const(a) for a in consts],
        out_specs=[pl.BlockSpec((tm, nh * kl), lambda i: (i, 0)),
                   pl.BlockSpec((tm, nh * rope), lambda i: (i, 0))],
        out_shape=(jax.ShapeDtypeStruct((N, nh * kl), BF16), jax.ShapeDtypeStruct((N, nh * rope), BF16)),
        compiler_params=pltpu.CompilerParams(dimension_semantics=("parallel",), vmem_limit_bytes=VMEM_LIMIT),
        name="absorb",
    )(q, *consts)


def _paged_body(pt_ref, a_ref, qpe_ref, wukt_ref, cnew_ref, knewt_ref, cache_c_ref, cache_kt_ref, o_ref,
                cbuf, kbuf, sem, s_ref, cb_ref, m_ref, l_ref, acc_ref, *, layer, n_pages, page, nh, nope):
    g = pl.program_id(0)
    ng = pl.num_programs(0)
    grp, rows, _ = a_ref.shape
    ch = PAGES_PER_CHUNK
    n_chunks = n_pages // ch

    def copies(gg, cc, slot):
        out = []
        for e in range(grp):
            for i in range(ch):
                pg = pt_ref[(gg * grp + e) * n_pages + cc * ch + i]
                out.append(pltpu.make_async_copy(cache_c_ref.at[layer, pg],
                                                 cbuf.at[slot, e, pl.ds(i * page, page)], sem.at[slot, 0]))
                out.append(pltpu.make_async_copy(cache_kt_ref.at[layer, pg],
                                                 kbuf.at[slot, e, :, pl.ds(i * page, page)], sem.at[slot, 1]))
        return out

    def fetch(gg, cc, slot):
        for n, c in enumerate(copies(gg, cc, slot)):
            c.start(priority=(n // 2 + n) % 2)

    nslot = cbuf.shape[0]
    ahead = nslot - 1

    @pl.when(g == 0)
    def _():
        for cc in range(ahead):
            fetch(0, cc, cc)

    m_ref[...] = jnp.full(m_ref.shape, NEG_INF, F32)
    l_ref[...] = jnp.zeros(l_ref.shape, F32)
    acc_ref[...] = jnp.zeros(acc_ref.shape, F32)

    def scores(e, ckv, kpet):
        cb = ckv.astype(BF16)
        kt = _dot_t(wukt_ref[...], cb)
        ss = jnp.sum((kt * kt).reshape(nope, nh, kt.shape[1]), axis=0)
        r = lax.rsqrt(ss * (1.0 / nope) + EPS)
        s = (_dot_t(a_ref[e], cb) * jnp.concatenate([r] * (rows // nh), axis=0)
             + _dot(qpe_ref[e], kpet.astype(BF16)))
        return s, cb

    def softmax_pv(e, s, cb):
        m_prev = m_ref[e]
        m_new = jnp.maximum(m_prev, jnp.max(s, axis=-1, keepdims=True))
        alpha = jnp.exp2(m_prev - m_new)
        pr = jnp.exp2(s - m_new)
        l_ref[e] = alpha * l_ref[e] + jnp.sum(pr, axis=-1, keepdims=True)
        acc_ref[e] = alpha * acc_ref[e] + _dot(pr.astype(BF16), cb)
        m_ref[e] = m_new

    def stage_scores(e, slot):
        s, cb = scores(e, cbuf[slot, e], kbuf[slot, e])
        s_ref[e] = s
        cb_ref[e] = cb

    def stage_softmax(e):
        softmax_pv(e, s_ref[e], cb_ref[e])

    def chunk(c, first):
        seq = g * n_chunks + c
        slot = seq % nslot
        nxt = c + ahead
        wrap = nxt >= n_chunks
        fetch(jnp.where(wrap, jnp.minimum(g + 1, ng - 1), g), jnp.where(wrap, nxt - n_chunks, nxt),
              (seq + ahead) % nslot)
        for cp in copies(g, c, slot):
            cp.wait()
        for e in range(grp):
            stage_scores(e, slot)
            if e > 0 or not first:
                stage_softmax((e - 1) % grp)

    chunk(0, True)

    def later_chunk(c, carry):
        chunk(c, False)
        return carry

    lax.fori_loop(1, n_chunks, later_chunk, 0)
    stage_softmax(grp - 1)

    @pl.when(g == ng - 1)
    def _():
        for cc in range(ahead):
            for cp in copies(g, cc, (ng * n_chunks + cc) % nslot):
                cp.wait()

    tq = cnew_ref.shape[1]
    zpad = jnp.zeros((page - tq, cnew_ref.shape[2]), F32)
    qi = lax.broadcasted_iota(jnp.int32, (rows, page), 0) // nh
    kj = lax.broadcasted_iota(jnp.int32, (rows, page), 1)
    for e in range(grp):
        s, cb = scores(e, jnp.concatenate([cnew_ref[e], zpad], axis=0), knewt_ref[e])
        softmax_pv(e, jnp.where(kj <= qi, s, NEG_INF), cb)
        o_ref[e] = acc_ref[e] / l_ref[e]


def _paged_attn(page_table, a, qpe, cnew, knewt, cache_c, cache_kt, p, layer):
    Bd, rows, kl = a.shape
    nh, rope, nope = p["nh"], p["rope"], p["nope"]
    n_pages = page_table.shape[1]
    page = cache_c.shape[2]
    grp = math.gcd(PAGED_GROUP, Bd)
    assert n_pages % PAGES_PER_CHUNK == 0 and n_pages // PAGES_PER_CHUNK >= PAGED_SLOTS
    keys = PAGES_PER_CHUNK * page
    tq = cnew.shape[1]
    row_blk = lambda r, w: pl.BlockSpec((grp, r, w), lambda b, pt: (b, 0, 0))
    grid_spec = pltpu.PrefetchScalarGridSpec(
        num_scalar_prefetch=1,
        grid=(Bd // grp,),
        in_specs=[
            row_blk(rows, kl),
            row_blk(rows, rope),
            pl.BlockSpec(p["w_ukt_perm"].shape, lambda b, pt: (0, 0), pipeline_mode=pl.Buffered(1)),
            row_blk(tq, kl),
            row_blk(rope, page),
            pl.BlockSpec(memory_space=pl.ANY),
            pl.BlockSpec(memory_space=pl.ANY),
        ],
        out_specs=row_blk(rows, kl),
        scratch_shapes=[
            pltpu.VMEM((PAGED_SLOTS, grp, keys, kl), F32),
            pltpu.VMEM((PAGED_SLOTS, grp, rope, keys), F32),
            pltpu.SemaphoreType.DMA((PAGED_SLOTS, 2)),
            pltpu.VMEM((grp, rows, keys), F32),
            pltpu.VMEM((grp, keys, kl), BF16),
            pltpu.VMEM((grp, rows, 1), F32),
            pltpu.VMEM((grp, rows, 1), F32),
            pltpu.VMEM((grp, rows, kl), F32),
        ],
    )
    return pl.pallas_call(
        functools.partial(_paged_body, layer=layer, n_pages=n_pages, page=page, nh=nh, nope=nope),
        grid_spec=grid_spec,
        out_shape=jax.ShapeDtypeStruct((Bd, rows, kl), F32),
        compiler_params=pltpu.CompilerParams(dimension_semantics=("arbitrary",), vmem_limit_bytes=VMEM_LIMIT),
        name="paged_attn",
    )(page_table.reshape(-1), a, qpe, p["w_ukt_perm"], cnew, knewt, cache_c, cache_kt)


def _sproj_body(o_ref, w_ref, out_ref):
    out_ref[...] = _dot(o_ref[...].astype(BF16), w_ref[...])


def _sproj(o_flat, w_bd):
    N, K = o_flat.shape
    tm = min(TOKEN_TILE, N)
    return pl.pallas_call(
        _sproj_body,
        grid=(N // tm,),
        in_specs=[pl.BlockSpec((tm, K), lambda i: (i, 0)),
                  pl.BlockSpec(w_bd.shape, lambda i: (0, 0), pipeline_mode=pl.Buffered(1))],
        out_specs=pl.BlockSpec((tm, w_bd.shape[1]), lambda i: (i, 0)),
        out_shape=jax.ShapeDtypeStruct((N, w_bd.shape[1]), F32),
        compiler_params=pltpu.CompilerParams(dimension_semantics=("parallel",), vmem_limit_bytes=VMEM_LIMIT),
        name="sproj",
    )(o_flat, w_bd)


def _post_tail(y, x, attn, pe, refs, out_ref):
    (lng_ref, lnb_ref, goc_ref, goa_ref, woa_ref, woc_ref, gffn_ref, wg_ref, wu_ref, wd_ref,
     gple_ref, wpg_ref, wpp_ref) = refs
    n = y.shape[0]
    ng = POST_ROW_GROUPS if n % (POST_ROW_GROUPS * BF16_ROWS) == 0 else 1
    rows = [slice(r * (n // ng), (r + 1) * (n // ng)) for r in range(ng)]

    def mix(r):
        yr = y[r]
        mu = jnp.mean(yr, axis=-1, keepdims=True)
        yc = yr - mu
        yn = yc * lax.rsqrt(jnp.mean(yc * yc, axis=-1, keepdims=True) + EPS) * lng_ref[...] + lnb_ref[...]
        conv = yn * jax.nn.sigmoid(yn)
        x1 = x[r] + _dot(_rms(attn[r], goa_ref[...]).astype(BF16), woa_ref[...])
        return x1 + _dot(_rms(conv, goc_ref[...]).astype(BF16), woc_ref[...])

    def ffn_in(x1):
        h = _rms(x1, gffn_ref[...]).astype(BF16)
        return _dot(h, wg_ref[...]), _dot(h, wu_ref[...])

    def ffn_out(x1, gu):
        g, u = gu
        return x1 + _dot((g * jax.nn.sigmoid(g) * u).astype(BF16), wd_ref[...])

    def ple(r, x2):
        gate = jax.nn.sigmoid(_dot(_rms(x2, gple_ref[...]).astype(BF16), wpg_ref[...]))
        out_ref[0, r, :] = x2 + gate * _dot(pe[r].astype(BF16), wpp_ref[...])

    x1 = [mix(r) for r in rows]
    gu = [ffn_in(a) for a in x1]
    x2 = [ffn_out(a, b) for a, b in zip(x1, gu)]
    for r, a in zip(rows, x2):
        ple(r, a)


def _post_prompt_body(x_ref, attn_ref, pe_ref, glu_ref, prev_ref, cw_ref, cb_ref, *rest, taps):
    refs, out_ref, xs_ref, sh_ref, y_ref = rest[:-4], rest[-4], rest[-3], rest[-2], rest[-1]
    i = pl.program_id(1)
    tm = glu_ref.shape[1]
    xs_ref[0:CONV_HALO, :] = jnp.where(i > 0, prev_ref[0], 0.0)
    xs_ref[CONV_HALO:, :] = glu_ref[0]
    off = CONV_HALO - (taps - 1)
    span = sh_ref.shape[1] - SUBLANES * ((taps - 1) // SUBLANES)
    for h0 in range(0, tm, span):
        for b in range(min(SUBLANES, taps)):
            nrow = span + SUBLANES * ((taps - 1 - b) // SUBLANES)
            sh_ref[b, 0:nrow, :] = xs_ref[h0 + off + b:h0 + off + b + nrow, :]
        for r0 in range(0, span, CONV_ROWS):
            acc = jnp.broadcast_to(cb_ref[...], (CONV_ROWS, cb_ref.shape[1]))
            for k in range(taps):
                a, b = divmod(k, SUBLANES)
                acc = acc + cw_ref[k:k + 1, :] * sh_ref[b, r0 + SUBLANES * a:r0 + SUBLANES * a + CONV_ROWS, :]
            y_ref[h0 + r0:h0 + r0 + CONV_ROWS, :] = acc
    _post_tail(y_ref[...], x_ref[0], attn_ref[0], pe_ref[0], refs, out_ref)


def _post_sample_body(x_ref, attn_ref, pe_ref, xc_ref, cw_ref, cb_ref, *rest, taps):
    refs, out_ref = rest[:-1], rest[-1]
    nb, _, cc = xc_ref.shape
    tq = x_ref.shape[1] // nb
    acc = jnp.broadcast_to(cb_ref[...].reshape(1, 1, cc), (nb, tq, cc))
    for k in range(taps):
        acc = acc + cw_ref[k:k + 1, :].reshape(1, 1, cc) * xc_ref[:, k:k + tq, :]
    _post_tail(acc.reshape(nb * tq, cc), x_ref[0], attn_ref[0], pe_ref[0], refs, out_ref)


def _post(x, attn, pe, conv_in, p, sample):
    B, T, D = x.shape
    tm = min(TOKEN_TILE, T)
    cc, taps = p["cc"], p["taps"]
    const = lambda a: pl.BlockSpec(a.shape, lambda b, i: (0,) * a.ndim, pipeline_mode=pl.Buffered(1))
    tok = lambda w: pl.BlockSpec((1, tm, w), lambda b, i: (b, i, 0))
    consts = [p["conv_w"], p["conv_b"], p["ln_g"], p["ln_b"], p["g_oc"], p["g_oa"], p["w_oa"], p["w_oc"],
              p["g_ffn"], p["w_gate"], p["w_up"], p["w_down"], p["g_ple"], p["w_pg"], p["w_pp"]]
    if sample:
        tq = conv_in.shape[1] - (taps - 1)
        nb = tm // tq
        conv_specs = [pl.BlockSpec((nb,) + conv_in.shape[1:], lambda b, i: (i, 0, 0))]
        conv_args = [conv_in]
        body = functools.partial(_post_sample_body, taps=taps)
        scratch = []
    else:
        per = tm // CONV_HALO
        conv_specs = [tok(cc),
                      pl.BlockSpec((1, CONV_HALO, cc), lambda b, i: (b, jnp.maximum(i * per - 1, 0), 0))]
        conv_args = [conv_in, conv_in]
        body = functools.partial(_post_prompt_body, taps=taps)
        span = max(CONV_ROWS, tm // 2)
        scratch = [pltpu.VMEM((tm + CONV_HALO, cc), F32),
                   pltpu.VMEM((SUBLANES, span + SUBLANES * ((taps - 1) // SUBLANES), cc), F32),
                   pltpu.VMEM((tm, cc), F32)]
    return pl.pallas_call(
        body,
        grid=(B, T // tm),
        in_specs=[tok(D), tok(attn.shape[2]), tok(pe.shape[2])] + conv_specs + [const(a) for a in consts],
        out_specs=tok(D),
        out_shape=jax.ShapeDtypeStruct((B, T, D), F32),
        scratch_shapes=scratch,
        compiler_params=pltpu.CompilerParams(
            dimension_semantics=("parallel", "parallel"), vmem_limit_bytes=VMEM_LIMIT),
        name="post_sample" if sample else "post_prompt",
    )(x, attn, pe, *conv_args, *consts)


def _rope_table(pos, rope):
    inv = 1.0 / (ROPE_THETA ** (np.arange(0, rope, 2, dtype=np.float64) / rope))
    ang = np.asarray(pos, np.float64)[:, None] * inv[None, :]
    cos, sin = np.cos(ang), np.sin(ang)
    cos2 = np.concatenate([cos, cos], axis=1)
    sin2 = np.concatenate([-sin, sin], axis=1)
    return np.concatenate([cos2, sin2, cos2, sin2], axis=1).astype(np.float32)


def _swap_halves(w, axis):
    a, b = jnp.split(w, 2, axis=axis)
    return jnp.concatenate([b, a], axis=axis)


def _layer_params(i, norm_mix, w_in, q_lora_norm, w_uq, kv_norm, w_uk, w_uv, q_nope_norm, k_nope_norm,
                  q_pe_norm, k_pe_norm, conv_w, conv_b, conv_ln_g, conv_ln_b, norm_out_attn, norm_out_conv,
                  w_out, norm_ffn, w_gate, w_up, w_down, norm_ple, w_ple_gate, w_ple_proj):
    ql, kl = q_lora_norm.shape[1], kv_norm.shape[1]
    nope, rope = q_nope_norm.shape[1], q_pe_norm.shape[1]
    cc, taps = conv_w.shape[2], conv_w.shape[1]
    nh = w_uk.shape[2] // nope
    vh = w_uv.shape[2] // nh
    assert nope + 2 * rope == LANES and 4 * rope == LANES and 2 * vh == LANES and nh % 2 == 0
    assert taps - 1 <= CONV_HALO
    scale = LOG2E / math.sqrt(nope + rope)
    row = lambda v: v.reshape(1, -1).astype(F32)

    wi = w_in[i]
    w_q, w_kv, w_kpe, w_a, w_g = jnp.split(wi, [ql, ql + kl, ql + kl + rope, ql + kl + rope + cc], axis=1)
    w_kpe_sw = _swap_halves(w_kpe, 1)
    w_in_cat = jnp.concatenate([w_q, w_kv, w_a, w_g, w_kpe, w_kpe_sw, w_kpe, w_kpe_sw], axis=1).astype(BF16)

    wq3 = w_uq[i].reshape(ql, nh, nope + rope)
    wq_pe = wq3[:, :, nope:]
    w_uq_cat = jnp.concatenate([wq3[:, :, :nope], wq_pe, _swap_halves(wq_pe, 2)], axis=2)
    w_uq_cat = w_uq_cat.reshape(ql, nh * LANES).astype(BF16)
    gq = jnp.concatenate([q_nope_norm[i], q_pe_norm[i], _swap_halves(q_pe_norm[i], 0)]) * scale
    g_qvec = row(jnp.tile(gq, nh))

    wk3 = w_uk[i].reshape(kl, nh, nope)
    w_uk_pad = jnp.concatenate([wk3, jnp.zeros((kl, nh, LANES - nope), F32)], axis=2).reshape(kl, nh * LANES)
    gk = jnp.concatenate([k_nope_norm[i], jnp.zeros((LANES - nope,), F32)])
    g_kvec = row(jnp.tile(gk, nh))
    g_kpe = row(jnp.tile(jnp.concatenate([k_pe_norm[i], _swap_halves(k_pe_norm[i], 0)]), 2))

    li = jnp.arange(LANES)
    blk_q = (jnp.where((li[:, None] < nope) & (li[None, :] < nope), 1.0 / nope, 0.0)
             + jnp.where((li[:, None] >= nope) & (li[:, None] < nope + rope) & (li[None, :] >= nope),
                         1.0 / rope, 0.0))
    blk_k = jnp.where((li[:, None] < nope) & (li[None, :] < nope), 1.0 / nope, 0.0)
    eye2 = jnp.eye(MXU_DIM // LANES, dtype=F32)
    s_q = jnp.kron(eye2, blk_q).astype(BF16)
    s_k = jnp.kron(eye2, blk_k).astype(BF16)
    s_kpe = jnp.where(li[:, None] < rope, 1.0 / rope, 0.0) * jnp.ones((1, LANES), F32)
    s_kpe = s_kpe.astype(BF16)

    w_ukt_pad = jnp.concatenate([jnp.transpose(wk3, (1, 2, 0)),
                                 jnp.zeros((nh, LANES - nope, kl), F32)], axis=1).astype(BF16)
    w_ukt_perm = jnp.transpose(wk3, (2, 1, 0)).reshape(nope * nh, kl).astype(BF16)
    src = jnp.arange(nh * LANES)
    lane_in = src % LANES
    dst = (src // LANES) * rope + (lane_in - nope) % rope
    p_mat = jnp.where((lane_in >= nope)[:, None] & (dst[:, None] == jnp.arange(nh * rope)[None, :]), 1.0, 0.0)
    p_mat = p_mat.astype(BF16)
    wv3 = w_uv[i].reshape(kl, nh, vh)
    w_bd = jnp.einsum("chv,hg->hcgv", wv3, jnp.eye(nh, dtype=F32)).reshape(nh * kl, nh * vh).astype(BF16)

    mw = norm_out_attn.shape[1]
    return dict(
        nh=nh, ql=ql, kl=kl, cc=cc, rope=rope, nope=nope, vh=vh, vw=nh * vh, taps=taps,
        g_mix=row(norm_mix[i]), w_in=w_in_cat, g_ql=row(q_lora_norm[i]), w_uq=w_uq_cat, g_qvec=g_qvec, s_q=s_q,
        g_kv=row(kv_norm[i]), w_uk_pad=w_uk_pad.astype(BF16), w_uvt=w_uv[i].T.astype(BF16), g_kvec=g_kvec, s_k=s_k, g_kpe=g_kpe, s_kpe=s_kpe,
        w_ukt_pad=w_ukt_pad, w_ukt_perm=w_ukt_perm, p_mat=p_mat, w_bd=w_bd,
        conv_w=conv_w[i].astype(F32), conv_b=row(conv_b[i]), ln_g=row(conv_ln_g[i]), ln_b=row(conv_ln_b[i]),
        g_oc=row(norm_out_conv[i]), g_oa=row(norm_out_attn[i]),
        w_oa=w_out[i][:mw].astype(BF16), w_oc=w_out[i][mw:].astype(BF16),
        g_ffn=row(norm_ffn[i]), w_gate=w_gate[i].astype(BF16), w_up=w_up[i].astype(BF16),
        w_down=w_down[i].astype(BF16), g_ple=row(norm_ple[i]), w_pg=w_ple_gate[i].astype(BF16),
        w_pp=w_ple_proj[i].astype(BF16),
    )


def kernel(x_prompt, x_sample, cache_ckv, cache_kpe, state_conv, page_table, p_prompt, p_sample, norm_mix, w_in, q_lora_norm, w_uq, kv_norm, w_uk, w_uv, q_nope_norm, k_nope_norm, q_pe_norm, k_pe_norm, conv_w, conv_b, conv_ln_g, conv_ln_b, norm_out_attn, norm_out_conv, w_out, norm_ffn, w_gate, w_up, w_down, norm_ple, w_ple_gate, w_ple_proj):
    B, T, D = x_prompt.shape
    Bd, Tq, _ = x_sample.shape
    depth = w_in.shape[0]
    past_len = page_table.shape[1] * cache_ckv.shape[2]
    rope = q_pe_norm.shape[1]
    taps = conv_w.shape[1]
    tab_p = _rope_table(np.arange(T), rope)
    tab_s = np.tile(_rope_table(past_len + np.arange(Tq), rope), (Bd, 1))

    xp, xs = x_prompt, x_sample.reshape(1, Bd * Tq, D)
    outs = [[] for _ in range(6)]
    for i in range(depth):
        p = _layer_params(i, norm_mix, w_in, q_lora_norm, w_uq, kv_norm, w_uk, w_uv, q_nope_norm, k_nope_norm,
                          q_pe_norm, k_pe_norm, conv_w, conv_b, conv_ln_g, conv_ln_b, norm_out_attn,
                          norm_out_conv, w_out, norm_ffn, w_gate, w_up, w_down, norm_ple, w_ple_gate, w_ple_proj)
        nh, kl, cc = p["nh"], p["kl"], p["cc"]
        q, k, vt, ckv, kpe, glu = _in_proj(xp, tab_p, p, BF16)
        attn = _flash(q, k, vt, p["vh"])
        xp = _post(xp, attn, p_prompt[i], glu, p, sample=False)
        outs[0].append(ckv)
        outs[1].append(kpe)
        outs[2].append(glu[:, T - (taps - 1):, :])
        q_s, _, _, ckv_s, kpe_s, glu_s = _in_proj(xs, tab_s, p, F32)
        a_s, qpe_s = _absorb(q_s[0], p)
        ckv_s = ckv_s.reshape(Bd, Tq, kl)
        kpe_s = kpe_s.reshape(Bd, Tq, rope)
        page = cache_kpe.shape[2]
        knewt = jnp.pad(jnp.swapaxes(kpe_s, 1, 2), ((0, 0), (0, 0), (0, page - Tq)))
        o_lat = _paged_attn(page_table, a_s.reshape(Bd, Tq * nh, kl), qpe_s.reshape(Bd, Tq * nh, rope),
                            ckv_s, knewt, cache_ckv, jnp.swapaxes(cache_kpe, 2, 3), p, i)
        attn_s = _sproj(o_lat.reshape(Bd * Tq, nh * kl), p["w_bd"])
        xc = jnp.concatenate([state_conv[i], glu_s.reshape(Bd, Tq, cc)], axis=1)
        xs = _post(xs, attn_s[None], p_sample[i].reshape(1, Bd * Tq, -1), xc, p, sample=True)
        outs[3].append(ckv_s)
        outs[4].append(kpe_s)
        outs[5].append(xc[:, Tq:, :])
    return (xp, xs.reshape(Bd, Tq, D), jnp.stack(outs[0]), jnp.stack(outs[1]), jnp.stack(outs[2]),
            jnp.stack(outs[3]), jnp.stack(outs[4]), jnp.stack(outs[5]))
```
